```python
import math
import jax, jax.numpy as jnp
from jax import lax
import numpy as np

D_MODEL = 2048
BATCH = 2
SEQ = 4096
DEPTH = 2
DEC_BATCH = 128
DEC_SEQ = 4
PAST_LEN = 2048
PAGE_SIZE = 128

HD_A = 128
W_A = D_MODEL // 2
N_HEADS_A = W_A // HD_A
HD_B = 64
W_B = D_MODEL // 4
N_HEADS_B = W_B // (2 * HD_B)
W_C = D_MODEL // 4
C_GROUP = 16
G_C = W_C // C_GROUP
P_STATE = 64
D_MIX = W_A + W_B + W_C
SPLIT_SIZES = (W_A, W_A, W_A, N_HEADS_A, W_A, W_B, W_B, W_B, W_B, W_C, W_C)
N_IN = 4 * W_A + N_HEADS_A + 4 * W_B + 2 * W_C

Q_BLOCK = 128
N_BUCKETS = 32
REL_MAX_DIST = 128
EPS = 1e-6
POOL_NUM = 5
POOL_DEN = 4

kernel_name = "hymba_fox_diff_s5_step"


def rms_norm(x, g):
    xf = x.astype(jnp.float32)
    y = xf * lax.rsqrt(jnp.mean(xf * xf, axis=-1, keepdims=True) + EPS)
    return (y * g.astype(jnp.float32)).astype(x.dtype)


def rel_bucket(d):
    max_exact = N_BUCKETS // 2
    df = jnp.maximum(d, 1).astype(jnp.float32)
    large = max_exact + (jnp.log(df / max_exact) / math.log(REL_MAX_DIST / max_exact)
                         * (N_BUCKETS - max_exact)).astype(jnp.int32)
    large = jnp.minimum(large, N_BUCKETS - 1)
    return jnp.where(d < max_exact, d, large)


def project(x, lw):
    B, T = x.shape[:2]
    h = rms_norm(x, lw["norm_gain"])
    z = jnp.einsum('btd,dn->btn', h, lw["w_in"])
    parts = []
    off = 0
    for s in SPLIT_SIZES:
        parts.append(z[..., off:off + s])
        off += s
    fq, fk, fv, ff, fgate, dq, dk, dv, dgate, su, sgate = parts
    fox_q = rms_norm(fq.reshape(B, T, N_HEADS_A, HD_A), lw["fox_q_gain"])
    fox_k = rms_norm(fk.reshape(B, T, N_HEADS_A, HD_A), lw["fox_k_gain"])
    fox_v = fv.reshape(B, T, N_HEADS_A, HD_A)
    fox_logf = jax.nn.log_sigmoid(ff.astype(jnp.float32) + lw["fox_b_f"].astype(jnp.float32))
    diff_q = rms_norm(dq.reshape(B, T, N_HEADS_B, 2, HD_B), lw["diff_q_gain"])
    diff_k = rms_norm(dk.reshape(B, T, N_HEADS_B, 2, HD_B), lw["diff_k_gain"])
    diff_v = dv.reshape(B, T, N_HEADS_B, 2 * HD_B)
    return fox_q, fox_k, fox_v, fox_logf, fgate, diff_q, diff_k, diff_v, dgate, su, sgate


def fox_attend(q, k, v, c_q, c_k, qpos, kpos):
    s = jnp.einsum('bqhd,bkhd->bhqk', q, k).astype(jnp.float32) * (HD_A ** -0.5)
    s = s + jnp.swapaxes(c_q, 1, 2)[..., :, None] - jnp.swapaxes(c_k, 1, 2)[..., None, :]
    s = jnp.where(kpos[None, :] <= qpos[:, None], s, -jnp.inf)
    p = jax.nn.softmax(s, axis=-1)
    return jnp.einsum('bhqk,bkhd->bqhd', p.astype(v.dtype), v)


def diff_attend(q, k, v, qpos, kpos, rel_bias, lam):
    s = jnp.einsum('bqhmd,bkhmd->bmhqk', q, k).astype(jnp.float32) * (HD_B ** -0.5)
    bucket = rel_bucket(jnp.maximum(qpos[:, None] - kpos[None, :], 0))
    bias = jnp.transpose(rel_bias[bucket].astype(jnp.float32), (2, 0, 1))
    s = s + bias
    s = jnp.where(kpos[None, :] <= qpos[:, None], s, -jnp.inf)
    p = jax.nn.softmax(s, axis=-1)
    a = p[:, 0] - lam * p[:, 1]
    return jnp.einsum('bhqk,bkhe->bqhe', a.astype(v.dtype), v)


def diff_lambda(lw, lambda_init):
    e1 = jnp.exp(jnp.sum(lw["diff_lambda_q1"].astype(jnp.float32) * lw["diff_lambda_k1"].astype(jnp.float32)))
    e2 = jnp.exp(jnp.sum(lw["diff_lambda_q2"].astype(jnp.float32) * lw["diff_lambda_k2"].astype(jnp.float32)))
    return e1 - e2 + lambda_init


def s5_mix(u, h0, lw):
    B, T = u.shape[:2]
    uf = u.astype(jnp.float32)
    lam = lax.complex(lw["s5_lambda_re"].astype(jnp.float32), lw["s5_lambda_im"].astype(jnp.float32))
    step = jnp.exp(lw["s5_log_step"].astype(jnp.float32))[:, None]
    lam_bar = jnp.exp(lam * step)
    b_mat = lax.complex(lw["s5_b_re"].astype(jnp.float32), lw["s5_b_im"].astype(jnp.float32))
    c_mat = lax.complex(lw["s5_c_re"].astype(jnp.float32), lw["s5_c_im"].astype(jnp.float32))
    b_bar = ((lam_bar - 1.0) / lam)[..., None] * b_mat
    bu = jnp.einsum('gpc,btgc->btgp', b_bar, uf.reshape(B, T, G_C, C_GROUP).astype(jnp.complex64))
    bu = bu.at[:, 0].add(lam_bar[None] * h0)
    a = jnp.broadcast_to(lam_bar, bu.shape)

    def combine(e1, e2):
        a1, b1 = e1
        a2, b2 = e2
        return a1 * a2, a2 * b1 + b2

    _, h = lax.associative_scan(combine, (a, bu), axis=1)
    y = jnp.einsum('gcp,btgp->btgc', c_mat, h).real.reshape(B, T, W_C) + lw["s5_d"].astype(jnp.float32) * uf
    z = jax.nn.gelu(y)
    out = z * jax.nn.sigmoid(jnp.einsum('btc,ce->bte', z, lw["s5_w_glu"].astype(jnp.float32))
                             + lw["s5_b_glu"].astype(jnp.float32))
    return out.astype(u.dtype), h[:, -1]


def merge(x, fox_o, fgate, diff_o, dgate, s5_o, sgate, w_out):
    B, T = x.shape[:2]
    m = jnp.concatenate([fox_o.reshape(B, T, W_A) * jax.nn.silu(fgate),
                         diff_o.reshape(B, T, W_B) * jax.nn.silu(dgate),
                         s5_o * jax.nn.silu(sgate)], axis=-1)
    return x + jnp.einsum('btm,md->btd', m, w_out)


def prompt_layer(x, lw, rel_bias, lambda_init):
    B, T = x.shape[:2]
    fq, fk, fv, flogf, fgate, dq, dk, dv, dgate, su, sgate = project(x, lw)
    c = jnp.cumsum(flogf, axis=1)
    kpos = jnp.arange(T)
    lam = diff_lambda(lw, lambda_init)
    n_blocks = T // Q_BLOCK

    def block(i):
        start = i * Q_BLOCK
        qpos = start + jnp.arange(Q_BLOCK)
        fq_i = lax.dynamic_slice_in_dim(fq, start, Q_BLOCK, axis=1)
        c_i = lax.dynamic_slice_in_dim(c, start, Q_BLOCK, axis=1)
        dq_i = lax.dynamic_slice_in_dim(dq, start, Q_BLOCK, axis=1)
        return (fox_attend(fq_i, fk, fv, c_i, c, qpos, kpos),
                diff_attend(dq_i, dk, dv, qpos, kpos, rel_bias, lam))

    fo, do = lax.map(block, jnp.arange(n_blocks))
    fo = jnp.moveaxis(fo, 0, 1).reshape(B, T, N_HEADS_A, HD_A)
    do = jnp.moveaxis(do, 0, 1).reshape(B, T, N_HEADS_B, 2 * HD_B)
    do = rms_norm(do, lw["diff_subln_gain"]) * (1.0 - lambda_init)
    h0 = jnp.zeros((B, G_C, P_STATE), jnp.complex64)
    so, h_last = s5_mix(su, h0, lw)
    y = merge(x, fo, fgate, do, dgate, so, sgate, lw["w_out"])
    return y, (fk, fv, flogf, dk.reshape(B, T, N_HEADS_B, 2 * HD_B), dv, h_last.real, h_last.imag)


def sample_layer(x, lw, rel_bias, lambda_init, pk, pv, plogf, pdk, pdv, h_re, h_im):
    B, S = x.shape[:2]
    PL = pk.shape[1]
    fq, fk, fv, flogf, fgate, dq, dk, dv, dgate, su, sgate = project(x, lw)
    k_all = jnp.concatenate([pk.astype(fk.dtype), fk], axis=1)
    v_all = jnp.concatenate([pv.astype(fv.dtype), fv], axis=1)
    c_all = jnp.cumsum(jnp.concatenate([plogf.astype(jnp.float32), flogf], axis=1), axis=1)
    qpos = PL + jnp.arange(S)
    kpos = jnp.arange(PL + S)
    fo = fox_attend(fq, k_all, v_all, c_all[:, PL:], c_all, qpos, kpos)
    dk_all = jnp.concatenate([pdk.reshape(B, PL, N_HEADS_B, 2, HD_B).astype(dk.dtype), dk], axis=1)
    dv_all = jnp.concatenate([pdv.astype(dv.dtype), dv], axis=1)
    lam = diff_lambda(lw, lambda_init)
    do = diff_attend(dq, dk_all, dv_all, qpos, kpos, rel_bias, lam)
    do = rms_norm(do, lw["diff_subln_gain"]) * (1.0 - lambda_init)
    h0 = lax.complex(h_re.astype(jnp.float32), h_im.astype(jnp.float32))
    so, h_last = s5_mix(su, h0, lw)
    y = merge(x, fo, fgate, do, dgate, so, sgate, lw["w_out"])
    return y, (fk, fv, flogf, dk.reshape(B, S, N_HEADS_B, 2 * HD_B), dv, h_last.real, h_last.imag)


def setup_inputs(seed: int = 0) -> dict:
    key = jax.random.key(seed)
    ks = jax.random.split(key, 40)
    f32 = jnp.float32
    n_pages = PAST_LEN // PAGE_SIZE
    n_pool = (DEC_BATCH * n_pages * POOL_NUM) // POOL_DEN

    def nrm(k, shape, scale):
        return scale * jax.random.normal(k, shape, f32)

    inp = {}
    inp["x_prompt"] = nrm(ks[0], (BATCH, SEQ, D_MODEL), 1.0)
    inp["x_sample"] = nrm(ks[1], (DEC_BATCH, DEC_SEQ, D_MODEL), 1.0)
    inp["cache_fox_k"] = nrm(ks[2], (DEPTH, n_pool, PAGE_SIZE, N_HEADS_A, HD_A), 1.0)
    inp["cache_fox_v"] = nrm(ks[3], (DEPTH, n_pool, PAGE_SIZE, N_HEADS_A, HD_A), 1.0)
    inp["cache_fox_logf"] = jax.nn.log_sigmoid(3.0 + nrm(ks[4], (DEPTH, n_pool, PAGE_SIZE, N_HEADS_A), 0.5))
    inp["cache_diff_k"] = nrm(ks[5], (DEPTH, n_pool, PAGE_SIZE, N_HEADS_B, 2 * HD_B), 1.0)
    inp["cache_diff_v"] = nrm(ks[6], (DEPTH, n_pool, PAGE_SIZE, N_HEADS_B, 2 * HD_B), 1.0)
    inp["state_s5_re"] = nrm(ks[7], (DEPTH, DEC_BATCH, G_C, P_STATE), 0.3)
    inp["state_s5_im"] = nrm(ks[8], (DEPTH, DEC_BATCH, G_C, P_STATE), 0.3)
    inp["page_table"] = jax.random.permutation(ks[9], n_pool)[:DEC_BATCH * n_pages].reshape(
        DEC_BATCH, n_pages).astype(jnp.int32)
    inp["norm_gain"] = 1.0 + nrm(ks[10], (DEPTH, D_MODEL), 0.05)
    inp["w_in"] = nrm(ks[11], (DEPTH, D_MODEL, N_IN), D_MODEL ** -0.5)
    inp["fox_b_f"] = 3.0 + nrm(ks[12], (DEPTH, N_HEADS_A), 0.5)
    inp["fox_q_gain"] = 1.0 + nrm(ks[13], (DEPTH, HD_A), 0.05)
    inp["fox_k_gain"] = 1.0 + nrm(ks[14], (DEPTH, HD_A), 0.05)
    inp["diff_q_gain"] = 1.0 + nrm(ks[15], (DEPTH, HD_B), 0.05)
    inp["diff_k_gain"] = 1.0 + nrm(ks[16], (DEPTH, HD_B), 0.05)
    inp["diff_lambda_q1"] = nrm(ks[17], (DEPTH, HD_B), 0.1)
    inp["diff_lambda_k1"] = nrm(ks[18], (DEPTH, HD_B), 0.1)
    inp["diff_lambda_q2"] = nrm(ks[19], (DEPTH, HD_B), 0.1)
    inp["diff_lambda_k2"] = nrm(ks[20], (DEPTH, HD_B), 0.1)
    inp["diff_subln_gain"] = 1.0 + nrm(ks[21], (DEPTH, 2 * HD_B), 0.05)
    inp["rel_bias"] = nrm(ks[22], (N_BUCKETS, N_HEADS_B), 0.1)
    inp["s5_lambda_re"] = -0.5 + nrm(ks[23], (DEPTH, G_C, P_STATE), 0.01)
    inp["s5_lambda_im"] = math.pi * jnp.arange(P_STATE, dtype=f32)[None, None, :] + nrm(
        ks[24], (DEPTH, G_C, P_STATE), 0.01)
    inp["s5_b_re"] = nrm(ks[25], (DEPTH, G_C, P_STATE, C_GROUP), (2 * C_GROUP) ** -0.5)
    inp["s5_b_im"] = nrm(ks[26], (DEPTH, G_C, P_STATE, C_GROUP), (2 * C_GROUP) ** -0.5)
    inp["s5_c_re"] = nrm(ks[27], (DEPTH, G_C, C_GROUP, P_STATE), (2 * P_STATE) ** -0.5)
    inp["s5_c_im"] = nrm(ks[28], (DEPTH, G_C, C_GROUP, P_STATE), (2 * P_STATE) ** -0.5)
    inp["s5_d"] = nrm(ks[29], (DEPTH, W_C), 1.0)
    inp["s5_log_step"] = jax.random.uniform(ks[30], (DEPTH, G_C), f32, math.log(1e-3), math.log(1e-1))
    inp["s5_w_glu"] = nrm(ks[31], (DEPTH, W_C, W_C), W_C ** -0.5)
    inp["s5_b_glu"] = nrm(ks[32], (DEPTH, W_C), 0.01)
    inp["w_out"] = nrm(ks[33], (DEPTH, D_MIX, D_MODEL), D_MIX ** -0.5)
    return inp


def reference(x_prompt, x_sample, cache_fox_k, cache_fox_v, cache_fox_logf, cache_diff_k, cache_diff_v,
              state_s5_re, state_s5_im, page_table, norm_gain, w_in, fox_b_f, fox_q_gain, fox_k_gain,
              diff_q_gain, diff_k_gain, diff_lambda_q1, diff_lambda_k1, diff_lambda_q2, diff_lambda_k2,
              diff_subln_gain, rel_bias, s5_lambda_re, s5_lambda_im, s5_b_re, s5_b_im, s5_c_re, s5_c_im,
              s5_d, s5_log_step, s5_w_glu, s5_b_glu, w_out):
    db = x_sample.shape[0]
    past_len = page_table.shape[1] * PAGE_SIZE
    xp = x_prompt
    xs = x_sample
    st_p = []
    st_s = []
    for l in range(DEPTH):
        lw = {
            "norm_gain": norm_gain[l], "w_in": w_in[l], "fox_b_f": fox_b_f[l],
            "fox_q_gain": fox_q_gain[l], "fox_k_gain": fox_k_gain[l],
            "diff_q_gain": diff_q_gain[l], "diff_k_gain": diff_k_gain[l],
            "diff_lambda_q1": diff_lambda_q1[l], "diff_lambda_k1": diff_lambda_k1[l],
            "diff_lambda_q2": diff_lambda_q2[l], "diff_lambda_k2": diff_lambda_k2[l],
            "diff_subln_gain": diff_subln_gain[l],
            "s5_lambda_re": s5_lambda_re[l], "s5_lambda_im": s5_lambda_im[l],
            "s5_b_re": s5_b_re[l], "s5_b_im": s5_b_im[l], "s5_c_re": s5_c_re[l], "s5_c_im": s5_c_im[l],
            "s5_d": s5_d[l], "s5_log_step": s5_log_step[l], "s5_w_glu": s5_w_glu[l], "s5_b_glu": s5_b_glu[l],
            "w_out": w_out[l],
        }
        lambda_init = 0.8 - 0.6 * math.exp(-0.3 * l)
        xp, sp = prompt_layer(xp, lw, rel_bias, lambda_init)
        pk = cache_fox_k[l, page_table].reshape(db, past_len, N_HEADS_A, HD_A)
        pv = cache_fox_v[l, page_table].reshape(db, past_len, N_HEADS_A, HD_A)
        plf = cache_fox_logf[l, page_table].reshape(db, past_len, N_HEADS_A)
        pdk = cache_diff_k[l, page_table].reshape(db, past_len, N_HEADS_B, 2 * HD_B)
        pdv = cache_diff_v[l, page_table].reshape(db, past_len, N_HEADS_B, 2 * HD_B)
        xs, ss = sample_layer(xs, lw, rel_bias, lambda_init, pk, pv, plf, pdk, pdv,
                              state_s5_re[l], state_s5_im[l])
        st_p.append(sp)
        st_s.append(ss)
    fk_p, fv_p, flf_p, dk_p, dv_p, sre_p, sim_p = [jnp.stack([s[i] for s in st_p]) for i in range(7)]
    fk_s, fv_s, flf_s, dk_s, dv_s, sre_s, sim_s = [jnp.stack([s[i] for s in st_s]) for i in range(7)]
    return (xp, xs, fk_p, fv_p, flf_p, dk_p, dv_p, sre_p, sim_p, fk_s, fv_s, flf_s, dk_s, dv_s, sre_s, sim_s)
```

```python
import functools
import math

import jax
import jax.numpy as jnp
from jax import lax
from jax.experimental import pallas as pl
from jax.experimental.pallas import tpu as pltpu

F32 = jnp.float32
BF16 = jnp.bfloat16
HIGHEST = lax.Precision.HIGHEST

EPS = 1e-6
NEG = -1e30
LANES = 128
SUBLANES = 8
PAGE = 128
N_BUCKETS = 32
REL_MAX_DIST = 128
VMEM_LIMIT = 56 * 1024 * 1024

_NT = (((1,), (1,)), ((), ()))


def _params(n_axes, vmem=VMEM_LIMIT):
    return pltpu.CompilerParams(dimension_semantics=("arbitrary",) * n_axes, vmem_limit_bytes=vmem)


def _silu(x):
    return x / (1.0 + jnp.exp(-x))


def _sigmoid(x):
    return 1.0 / (1.0 + jnp.exp(-x))


def _log_sigmoid(x):
    return jnp.minimum(x, 0.0) - jnp.log1p(jnp.exp(-jnp.abs(x)))


def _gelu_tanh(x):
    return 0.5 * x * (1.0 + jnp.tanh(math.sqrt(2.0 / math.pi) * (x + 0.044715 * (x * x * x))))


CUMSUM_CHUNK = 256


def _proj_body(x_ref, g_ref, w_ref, wf_ref, bf_ref, gains_ref, tri_ref,
               z_ref, logf_ref, c_ref, ct_ref, h_scr, carry_scr, *, tm, tn, seq_tiles, seg):
    m = pl.program_id(0)
    n = pl.program_id(1)

    @pl.when(n == 0)
    def _():
        x = x_ref[...]
        ms = jnp.mean(x * x, axis=-1, keepdims=True)
        hb = (x * lax.rsqrt(ms + EPS) * g_ref[...]).astype(BF16)
        h_scr[...] = hb
        logits = jnp.dot(hb, wf_ref[...], preferred_element_type=F32) + bf_ref[...]
        lf = _log_sigmoid(logits)
        logf_ref[...] = lf

        @pl.when(m % seq_tiles == 0)
        def _():
            carry_scr[...] = jnp.zeros_like(carry_scr)

        carry = carry_scr[0:1, :]
        for j in range(tm // CUMSUM_CHUNK):
            blk = lf[j * CUMSUM_CHUNK:(j + 1) * CUMSUM_CHUNK]
            cs = jnp.dot(tri_ref[...], blk, precision=HIGHEST, preferred_element_type=F32) + carry
            c_ref[j * CUMSUM_CHUNK:(j + 1) * CUMSUM_CHUNK, :] = cs
            carry = cs[CUMSUM_CHUNK - 1:CUMSUM_CHUNK, :]
        carry_scr[...] = jnp.broadcast_to(carry, carry_scr.shape)
        ct_ref[...] = c_ref[...].T[0:SUBLANES, :]

    acc = jnp.dot(h_scr[...], w_ref[...], preferred_element_type=F32)
    lane = lax.broadcasted_iota(jnp.int32, (tm, LANES), 1)

    def norm_full(row):
        gain = gains_ref[row:row + 1, :]
        for j in range(tn // LANES):
            y = acc[:, j * LANES:(j + 1) * LANES]
            ms = jnp.mean(y * y, axis=-1, keepdims=True)
            z_ref[:, j * LANES:(j + 1) * LANES] = y * lax.rsqrt(ms + EPS) * gain

    def norm_half(row):
        gain = gains_ref[row:row + 1, :]
        half = LANES // 2
        for j in range(tn // LANES):
            y = acc[:, j * LANES:(j + 1) * LANES]
            y2 = y * y
            lo = jnp.sum(jnp.where(lane < half, y2, 0.0), axis=-1, keepdims=True)
            hi = jnp.sum(jnp.where(lane >= half, y2, 0.0), axis=-1, keepdims=True)
            ms = jnp.where(lane < half, lo, hi) * (1.0 / half)
            z_ref[:, j * LANES:(j + 1) * LANES] = y * lax.rsqrt(ms + EPS) * gain

    fq0, fk0, fv0, dq0, dk0, dv0 = seg

    @pl.when(n < fk0)
    def _():
        norm_full(0)

    @pl.when((n >= fk0) & (n < fv0))
    def _():
        norm_full(1)

    @pl.when(n == dq0)
    def _():
        norm_half(2)

    @pl.when(n == dk0)
    def _():
        norm_half(3)

    @pl.when(((n >= fv0) & (n < dq0)) | (n >= dv0))
    def _():
        z_ref[...] = acc


def _proj(x, g, w_main, w_f, b_f, gains, tri, *, seq_len, w_a, w_b):
    rows, d = x.shape
    n_cols = w_main.shape[1]
    tm = min(1024, rows)
    tn = 512
    assert rows % tm == 0 and n_cols % tn == 0 and seq_len % tm == 0 and tm % CUMSUM_CHUNK == 0
    assert w_a % tn == 0 and w_b == tn
    seg = (0, w_a // tn, 2 * w_a // tn, 4 * w_a // tn, 4 * w_a // tn + 1, 4 * w_a // tn + 2)
    body = functools.partial(_proj_body, tm=tm, tn=tn, seq_tiles=seq_len // tm, seg=seg)
    return pl.pallas_call(
        body,
        grid=(rows // tm, n_cols // tn),
        in_specs=[
            pl.BlockSpec((tm, d), lambda m, n: (m, 0)),
            pl.BlockSpec((1, d), lambda m, n: (0, 0)),
            pl.BlockSpec((d, tn), lambda m, n: (0, n)),
            pl.BlockSpec((d, LANES), lambda m, n: (0, 0)),
            pl.BlockSpec((1, LANES), lambda m, n: (0, 0)),
            pl.BlockSpec((4, LANES), lambda m, n: (0, 0)),
            pl.BlockSpec((CUMSUM_CHUNK, CUMSUM_CHUNK), lambda m, n: (0, 0)),
        ],
        out_specs=[
            pl.BlockSpec((tm, tn), lambda m, n: (m, n)),
            pl.BlockSpec((tm, LANES), lambda m, n: (m, 0)),
            pl.BlockSpec((tm, LANES), lambda m, n: (m, 0)),
            pl.BlockSpec((SUBLANES, tm), lambda m, n: (0, m)),
        ],
        out_shape=[
            jax.ShapeDtypeStruct((rows, n_cols), F32),
            jax.ShapeDtypeStruct((rows, LANES), F32),
            jax.ShapeDtypeStruct((rows, LANES), F32),
            jax.ShapeDtypeStruct((SUBLANES, rows), F32),
        ],
        scratch_shapes=[pltpu.VMEM((tm, d), BF16), pltpu.VMEM((SUBLANES, LANES), F32)],
        compiler_params=_params(2),
        name="proj",
    )(x, g, w_main, w_f, b_f, gains, tri)


def _fox_body(q_ref, k_ref, v_ref, c_ref, ct_ref, gate_ref, o_ref, m_scr, l_scr, acc_scr, *, n_heads, hd, tq, tk):
    qi = pl.program_id(1)
    ki = pl.program_id(2)
    nk = pl.num_programs(2)
    scale = hd ** -0.5

    @pl.when(ki == 0)
    def _():
        m_scr[...] = jnp.full_like(m_scr, NEG)
        l_scr[...] = jnp.zeros_like(l_scr)
        acc_scr[...] = jnp.zeros_like(acc_scr)

    @pl.when(ki <= qi)
    def _():
        row = lax.broadcasted_iota(jnp.int32, (tq, tk), 0) + qi * tq
        col = lax.broadcasted_iota(jnp.int32, (tq, tk), 1) + ki * tk
        visible = col <= row
        for h in range(n_heads):
            sl = slice(h * hd, (h + 1) * hd)
            q = q_ref[:, sl].astype(BF16)
            k = k_ref[:, sl].astype(BF16)
            s = lax.dot_general(q, k, _NT, preferred_element_type=F32) * scale
            s = s + c_ref[:, h:h + 1] - ct_ref[h:h + 1, :]
            s = jnp.where(visible, s, NEG)
            m_prev = m_scr[h]
            m_new = jnp.maximum(m_prev, jnp.max(s, axis=-1, keepdims=True))
            alpha = jnp.exp(m_prev - m_new)
            p = jnp.exp(s - m_new[:, 0:1])
            l_scr[h] = alpha * l_scr[h] + jnp.sum(p, axis=-1, keepdims=True)
            pv = jnp.dot(p.astype(BF16), v_ref[:, sl].astype(BF16), preferred_element_type=F32)
            acc_scr[:, sl] = alpha * acc_scr[:, sl] + pv
            m_scr[h] = m_new

    @pl.when(ki == nk - 1)
    def _():
        for h in range(n_heads):
            sl = slice(h * hd, (h + 1) * hd)
            o = acc_scr[:, sl] / l_scr[h]
            o_ref[:, sl] = (o * _silu(gate_ref[:, sl])).astype(BF16)


def _fox_prompt(z, c, ct, *, batch, seq_len, n_heads, hd):
    w = n_heads * hd
    tq = tk = min(512, seq_len)
    nq = seq_len // tq
    body = functools.partial(_fox_body, n_heads=n_heads, hd=hd, tq=tq, tk=tk)
    kv_row = lambda b, qi, ki: b * nq + jnp.minimum(ki, qi)
    return pl.pallas_call(
        body,
        grid=(batch, nq, nq),
        in_specs=[
            pl.BlockSpec((tq, w), lambda b, qi, ki: (b * nq + qi, 0)),
            pl.BlockSpec((tk, w), lambda b, qi, ki: (kv_row(b, qi, ki), 1)),
            pl.BlockSpec((tk, w), lambda b, qi, ki: (kv_row(b, qi, ki), 2)),
            pl.BlockSpec((tq, LANES), lambda b, qi, ki: (b * nq + qi, 0)),
            pl.BlockSpec((SUBLANES, tk), lambda b, qi, ki: (0, kv_row(b, qi, ki))),
            pl.BlockSpec((tq, w), lambda b, qi, ki: (b * nq + qi, 3)),
        ],
        out_specs=pl.BlockSpec((tq, w), lambda b, qi, ki: (b * nq + qi, 0)),
        out_shape=jax.ShapeDtypeStruct((batch * seq_len, w), BF16),
        scratch_shapes=[
            pltpu.VMEM((n_heads, tq, LANES), F32),
            pltpu.VMEM((n_heads, tq, LANES), F32),
            pltpu.VMEM((tq, w), F32),
        ],
        compiler_params=_params(3),
        name="fox_prompt",
    )(z, z, z, c, ct, z)


def _rel_bucket(d):
    max_exact = N_BUCKETS // 2
    df = jnp.maximum(d, 1).astype(F32)
    large = max_exact + (jnp.log(df / max_exact) / math.log(REL_MAX_DIST / max_exact)
                         * (N_BUCKETS - max_exact)).astype(jnp.int32)
    large = jnp.minimum(large, N_BUCKETS - 1)
    return jnp.where(d < max_exact, d, large)


def _bias_lookup(rel_ref, d, h):
    bucket = _rel_bucket(jnp.maximum(d, 0))
    val = jnp.full(d.shape, rel_ref[0, h], F32)
    for b in range(1, N_BUCKETS):
        val = jnp.where(bucket == b, rel_ref[b, h], val)
    return jnp.where(d >= 0, val, NEG)


def _bias_tiles_body(rel_ref, o_ref, *, tq, tk):
    h = pl.program_id(0)
    kind = pl.program_id(1)
    row = lax.broadcasted_iota(jnp.int32, (tq, tk), 0)
    col = lax.broadcasted_iota(jnp.int32, (tq, tk), 1)
    o_ref[...] = _bias_lookup(rel_ref, kind * tq + row - col, h)


def _bias_tiles(rel_bias, *, n_heads, tq, tk):
    return pl.pallas_call(
        functools.partial(_bias_tiles_body, tq=tq, tk=tk),
        grid=(n_heads, 3),
        in_specs=[pl.BlockSpec(memory_space=pltpu.SMEM)],
        out_specs=pl.BlockSpec((None, None, tq, tk), lambda h, kind: (h, kind, 0, 0)),
        out_shape=jax.ShapeDtypeStruct((n_heads, 3, tq, tk), F32),
        compiler_params=_params(2),
        name="bias_tiles",
    )(rel_bias)


def _sample_bias_body(rel_ref, o_ref, *, n_q):
    kind = pl.program_id(0)
    pair = pl.program_id(1)
    rows = 4 * n_q
    r = lax.broadcasted_iota(jnp.int32, (rows, PAGE), 0)
    col = lax.broadcasted_iota(jnp.int32, (rows, PAGE), 1)
    i = r % n_q
    d_last = PAGE + i - col
    d_new = jnp.where(col < n_q, i - col, -1)
    d_far = jnp.full((rows, PAGE), 2 * PAGE, jnp.int32)
    d = jnp.where(kind == 0, d_last, jnp.where(kind == 1, d_new, d_far))
    lo = _bias_lookup(rel_ref, d, 2 * pair)
    hi = _bias_lookup(rel_ref, d, 2 * pair + 1)
    o_ref[...] = jnp.where(r < 2 * n_q, lo, hi)


def _sample_bias(rel_bias, *, n_heads, n_q):
    rows = 4 * n_q
    return pl.pallas_call(
        functools.partial(_sample_bias_body, n_q=n_q),
        grid=(3, n_heads // 2),
        in_specs=[pl.BlockSpec(memory_space=pltpu.SMEM)],
        out_specs=pl.BlockSpec((None, None, rows, PAGE), lambda kind, pair: (kind, pair, 0, 0)),
        out_shape=jax.ShapeDtypeStruct((3, n_heads // 2, rows, PAGE), F32),
        compiler_params=_params(2),
        name="sample_bias",
    )(rel_bias)


def _diff_lambda(lam_ref, lambda_init):
    lv = lam_ref[...]
    e1 = jnp.exp(jnp.sum(lv[0:1, :] * lv[1:2, :], axis=-1, keepdims=True))
    e2 = jnp.exp(jnp.sum(lv[2:3, :] * lv[3:4, :], axis=-1, keepdims=True))
    return e1 - e2 + lambda_init


def _diff_body(q_ref, k_ref, v_ref, bias_ref, gate_ref, lam_ref, sub_ref, o_ref,
               m_scr, l_scr, acc_scr, *, n_heads, hd, tq, tk, lambda_init):
    qi = pl.program_id(1)
    ki = pl.program_id(2)
    nk = pl.num_programs(2)
    hw = 2 * hd
    scale = hd ** -0.5

    @pl.when(ki == 0)
    def _():
        m_scr[...] = jnp.full_like(m_scr, NEG)
        l_scr[...] = jnp.zeros_like(l_scr)
        acc_scr[...] = jnp.zeros_like(acc_scr)

    @pl.when(ki <= qi)
    def _():
        lane = lax.broadcasted_iota(jnp.int32, (tq, hw), 1)
        for h in range(n_heads):
            sl = slice(h * hw, (h + 1) * hw)
            q = q_ref[:, sl]
            k = k_ref[:, sl].astype(BF16)
            v = v_ref[:, sl].astype(BF16)
            bias = bias_ref[h]
            for mp in range(2):
                qm = jnp.where((lane < hd) if mp == 0 else (lane >= hd), q, 0.0).astype(BF16)
                s = lax.dot_general(qm, k, _NT, preferred_element_type=F32) * scale + bias
                idx = 2 * h + mp
                m_prev = m_scr[idx]
                m_new = jnp.maximum(m_prev, jnp.max(s, axis=-1, keepdims=True))
                alpha = jnp.exp(m_prev - m_new)
                p = jnp.exp(s - m_new[:, 0:1])
                l_scr[idx] = alpha * l_scr[idx] + jnp.sum(p, axis=-1, keepdims=True)
                pv = jnp.dot(p.astype(BF16), v, preferred_element_type=F32)
                acc_scr[idx] = alpha * acc_scr[idx] + pv
                m_scr[idx] = m_new

    @pl.when(ki == nk - 1)
    def _():
        lam = _diff_lambda(lam_ref, lambda_init)
        for h in range(n_heads):
            sl = slice(h * hw, (h + 1) * hw)
            o = acc_scr[2 * h] / l_scr[2 * h] - lam * (acc_scr[2 * h + 1] / l_scr[2 * h + 1])
            ms = jnp.mean(o * o, axis=-1, keepdims=True)
            o = o * lax.rsqrt(ms + EPS) * sub_ref[...] * (1.0 - lambda_init)
            o_ref[:, sl] = (o * _silu(gate_ref[:, sl])).astype(BF16)


def _diff_prompt(z, bias_tiles, lam_vecs, sub_gain, *, batch, seq_len, n_heads, hd, col0, lambda_init):
    hw = 2 * hd
    w = n_heads * hw
    tq = tk = min(512, seq_len)
    nq = seq_len // tq
    cb = col0 // w
    body = functools.partial(_diff_body, n_heads=n_heads, hd=hd, tq=tq, tk=tk, lambda_init=lambda_init)
    kv_row = lambda b, qi, ki: b * nq + jnp.minimum(ki, qi)
    kind = lambda b, qi, ki: jnp.clip(qi - jnp.minimum(ki, qi), 0, 2)
    return pl.pallas_call(
        body,
        grid=(batch, nq, nq),
        in_specs=[
            pl.BlockSpec((tq, w), lambda b, qi, ki: (b * nq + qi, cb)),
            pl.BlockSpec((tk, w), lambda b, qi, ki: (kv_row(b, qi, ki), cb + 1)),
            pl.BlockSpec((tk, w), lambda b, qi, ki: (kv_row(b, qi, ki), cb + 2)),
            pl.BlockSpec((n_heads, None, tq, tk), lambda b, qi, ki: (0, kind(b, qi, ki), 0, 0)),
            pl.BlockSpec((tq, w), lambda b, qi, ki: (b * nq + qi, cb + 3)),
            pl.BlockSpec((4, hd), lambda b, qi, ki: (0, 0)),
            pl.BlockSpec((1, hw), lambda b, qi, ki: (0, 0)),
        ],
        out_specs=pl.BlockSpec((tq, w), lambda b, qi, ki: (b * nq + qi, 0)),
        out_shape=jax.ShapeDtypeStruct((batch * seq_len, w), BF16),
        scratch_shapes=[
            pltpu.VMEM((2 * n_heads, tq, LANES), F32),
            pltpu.VMEM((2 * n_heads, tq, LANES), F32),
            pltpu.VMEM((2 * n_heads, tq, hw), F32),
        ],
        compiler_params=_params(3),
        name="diff_prompt",
    )(z, z, z, bias_tiles, z, lam_vecs, sub_gain)


def _s5_prep_body(lre_ref, lim_ref, ls_ref, bre_ref, bim_ref, are_ref, aim_ref, bbre_ref, bbim_ref):
    lr = lre_ref[...]
    li = lim_ref[...]
    step = jnp.exp(ls_ref[...])
    mag = jnp.exp(lr * step)
    ar = mag * jnp.cos(li * step)
    ai = mag * jnp.sin(li * step)
    den = lr * lr + li * li
    nr = ar - 1.0
    cr = (nr * lr + ai * li) / den
    ci = (ai * lr - nr * li) / den
    br = bre_ref[...]
    bi = bim_ref[...]
    are_ref[...] = ar
    aim_ref[...] = ai
    bbre_ref[...] = cr * br - ci * bi
    bbim_ref[...] = cr * bi + ci * br


def _s5_prep(lam_re, lam_im, log_step, b_re, b_im):
    g, p, c = b_re.shape
    col = lambda a: a.reshape(g, p, 1)
    ls = jnp.broadcast_to(log_step.reshape(g, 1, 1), (g, p, 1))
    full3 = lambda shp: pl.BlockSpec(shp, lambda: (0, 0, 0))
    return pl.pallas_call(
        _s5_prep_body,
        in_specs=[full3((g, p, 1))] * 3 + [full3((g, p, c))] * 2,
        out_specs=[full3((g, p, 1))] * 2 + [full3((g, p, c))] * 2,
        out_shape=[jax.ShapeDtypeStruct((g, p, 1), F32)] * 2 + [jax.ShapeDtypeStruct((g, p, c), F32)] * 2,
        name="s5_prep",
    )(col(lam_re), col(lam_im), ls, b_re, b_im)


def _block_diag(blocks):
    g, r, c = blocks.shape
    eye = jnp.eye(g, dtype=blocks.dtype)
    return (blocks[:, :, None, :] * eye[:, None, :, None]).reshape(g * r, g * c)


S5_LANE_CHUNK = 256


def _s5_scan_body(u_ref, gate_ref, bre_ref, bim_ref, cre_ref, cim_ref, are_ref, aim_ref, d_ref, wg_ref, bg_ref,
                  o_ref, hre_ref, him_ref, pwr_scr, pwi_scr, hr_scr, hi_scr, cr_scr, ci_scr, *, tm, n_state):
    b = pl.program_id(0)
    i = pl.program_id(1)
    nt = pl.num_programs(1)
    n_levels = tm.bit_length() - 1

    @pl.when((b == 0) & (i == 0))
    def _():
        pwr_scr[0:1, :] = are_ref[...]
        pwi_scr[0:1, :] = aim_ref[...]
        for lv in range(n_levels):
            s = 1 << lv
            tr = pwr_scr[s - 1:s, :]
            ti = pwi_scr[s - 1:s, :]
            xr = pwr_scr[0:s, :]
            xi = pwi_scr[0:s, :]
            pwr_scr[s:2 * s, :] = xr * tr - xi * ti
            pwi_scr[s:2 * s, :] = xr * ti + xi * tr

    @pl.when(i == 0)
    def _():
        cr_scr[...] = jnp.zeros_like(cr_scr)
        ci_scr[...] = jnp.zeros_like(ci_scr)

    u = u_ref[...]
    ub = u.astype(BF16)
    hr_scr[...] = jnp.dot(ub, bre_ref[...], preferred_element_type=F32)
    hi_scr[...] = jnp.dot(ub, bim_ref[...], preferred_element_type=F32)

    row = lax.broadcasted_iota(jnp.int32, (tm, S5_LANE_CHUNK), 0)

    def chunk(j, carry):
        ln = pl.ds(pl.multiple_of(j * S5_LANE_CHUNK, S5_LANE_CHUNK), S5_LANE_CHUNK)
        hr = hr_scr[:, ln]
        hi = hi_scr[:, ln]
        for lv in range(n_levels):
            s = 1 << lv
            mr = pwr_scr[s - 1:s, ln]
            mi = pwi_scr[s - 1:s, ln]
            sr = jnp.where(row >= s, pltpu.roll(hr, s, 0), 0.0)
            si = jnp.where(row >= s, pltpu.roll(hi, s, 0), 0.0)
            hr, hi = hr + (mr * sr - mi * si), hi + (mr * si + mi * sr)
        pr = cr_scr[0:1, ln]
        pi = ci_scr[0:1, ln]
        wr = pwr_scr[:, ln]
        wi = pwi_scr[:, ln]
        hr, hi = hr + (wr * pr - wi * pi), hi + (wr * pi + wi * pr)
        hr_scr[:, ln] = hr
        hi_scr[:, ln] = hi
        cr_scr[:, ln] = jnp.broadcast_to(hr[tm - 1:tm, :], (SUBLANES, S5_LANE_CHUNK))
        ci_scr[:, ln] = jnp.broadcast_to(hi[tm - 1:tm, :], (SUBLANES, S5_LANE_CHUNK))
        return carry

    lax.fori_loop(0, n_state // S5_LANE_CHUNK, chunk, 0)

    y = (jnp.dot(hr_scr[...].astype(BF16), cre_ref[...], preferred_element_type=F32)
         - jnp.dot(hi_scr[...].astype(BF16), cim_ref[...], preferred_element_type=F32)
         + d_ref[...] * u)
    zz = _gelu_tanh(y)
    glu = zz * _sigmoid(jnp.dot(zz.astype(BF16), wg_ref[...], preferred_element_type=F32) + bg_ref[...])
    o_ref[...] = (glu * _silu(gate_ref[...])).astype(BF16)

    @pl.when(i == nt - 1)
    def _():
        hre_ref[...] = cr_scr[0:1, :]
        him_ref[...] = ci_scr[0:1, :]


def _s5_prompt(z, mats, d_vec, w_glu, b_glu, *, batch, seq_len, col0, w_c):
    bre, bim, cre, cim, are, aim = mats
    n_state = bre.shape[1]
    tm = min(512, seq_len)
    nt = seq_len // tm
    cb = col0 // w_c
    body = functools.partial(_s5_scan_body, tm=tm, n_state=n_state)
    const = lambda shp: pl.BlockSpec(shp, lambda b, i: (0,) * len(shp))
    return pl.pallas_call(
        body,
        grid=(batch, nt),
        in_specs=[
            pl.BlockSpec((tm, w_c), lambda b, i: (b * nt + i, cb)),
            pl.BlockSpec((tm, w_c), lambda b, i: (b * nt + i, cb + 1)),
            const((w_c, n_state)), const((w_c, n_state)),
            const((n_state, w_c)), const((n_state, w_c)),
            const((1, n_state)), const((1, n_state)),
            const((1, w_c)), const((w_c, w_c)), const((1, w_c)),
        ],
        out_specs=[
            pl.BlockSpec((tm, w_c), lambda b, i: (b * nt + i, 0)),
            pl.BlockSpec((None, 1, n_state), lambda b, i: (b, 0, 0)),
            pl.BlockSpec((None, 1, n_state), lambda b, i: (b, 0, 0)),
        ],
        out_shape=[
            jax.ShapeDtypeStruct((batch * seq_len, w_c), BF16),
            jax.ShapeDtypeStruct((batch, 1, n_state), F32),
            jax.ShapeDtypeStruct((batch, 1, n_state), F32),
        ],
        scratch_shapes=[
            pltpu.VMEM((tm, n_state), F32), pltpu.VMEM((tm, n_state), F32),
            pltpu.VMEM((tm, n_state), F32), pltpu.VMEM((tm, n_state), F32),
            pltpu.VMEM((SUBLANES, n_state), F32), pltpu.VMEM((SUBLANES, n_state), F32),
        ],
        compiler_params=_params(2),
        name="s5_prompt",
    )(z, z, bre.astype(BF16), bim.astype(BF16), cre.astype(BF16), cim.astype(BF16), are, aim,
      d_vec, w_glu.astype(BF16), b_glu)


def _s5_sample_body(u_ref, gate_ref, h0r_ref, h0i_ref, bre_ref, bim_ref, cre_ref, cim_ref, are_ref, aim_ref,
                    d_ref, wg_ref, bg_ref, o_ref, hre_ref, him_ref, *, n_steps, w_c):
    hr = h0r_ref[...]
    hi = h0i_ref[...]
    ar = are_ref[...]
    ai = aim_ref[...]
    for t in range(n_steps):
        sl = slice(t * w_c, (t + 1) * w_c)
        u = u_ref[:, sl]
        bur = jnp.dot(u, bre_ref[...], precision=HIGHEST, preferred_element_type=F32)
        bui = jnp.dot(u, bim_ref[...], precision=HIGHEST, preferred_element_type=F32)
        hr, hi = ar * hr - ai * hi + bur, ar * hi + ai * hr + bui
        y = (jnp.dot(hr.astype(BF16), cre_ref[...], preferred_element_type=F32)
             - jnp.dot(hi.astype(BF16), cim_ref[...], preferred_element_type=F32)
             + d_ref[...] * u)
        zz = _gelu_tanh(y)
        glu = zz * _sigmoid(jnp.dot(zz.astype(BF16), wg_ref[...], preferred_element_type=F32) + bg_ref[...])
        o_ref[:, sl] = (glu * _silu(gate_ref[:, sl])).astype(BF16)
    hre_ref[...] = hr
    him_ref[...] = hi


def _s5_sample(u, gate, h0r, h0i, mats, d_vec, w_glu, b_glu, *, n_steps, w_c):
    bre, bim, cre, cim, are, aim = mats
    n_seq = u.shape[0]
    n_state = bre.shape[1]
    body = functools.partial(_s5_sample_body, n_steps=n_steps, w_c=w_c)
    return pl.pallas_call(
        body,
        out_shape=[
            jax.ShapeDtypeStruct((n_seq, n_steps * w_c), BF16),
            jax.ShapeDtypeStruct((n_seq, n_state), F32),
            jax.ShapeDtypeStruct((n_seq, n_state), F32),
        ],
        compiler_params=pltpu.CompilerParams(vmem_limit_bytes=VMEM_LIMIT),
        name="s5_sample",
    )(u, gate, h0r, h0i, bre, bim, cre.astype(BF16), cim.astype(BF16), are, aim, d_vec, w_glu.astype(BF16), b_glu)


def _merge_body(x_ref, mf_ref, md_ref, ms_ref, w_ref, o_ref, *, w_a, w_b):
    acc = jnp.dot(mf_ref[...].astype(BF16), w_ref[0:w_a, :], preferred_element_type=F32)
    acc += jnp.dot(md_ref[...].astype(BF16), w_ref[w_a:w_a + w_b, :], preferred_element_type=F32)
    acc += jnp.dot(ms_ref[...].astype(BF16), w_ref[w_a + w_b:, :], preferred_element_type=F32)
    o_ref[...] = x_ref[...] + acc


def _merge(x, mf, md, ms, w_out_bf16):
    rows, d = x.shape
    w_a, w_b, w_c = mf.shape[1], md.shape[1], ms.shape[1]
    tm = min(512, rows)
    body = functools.partial(_merge_body, w_a=w_a, w_b=w_b)
    return pl.pallas_call(
        body,
        grid=(rows // tm,),
        in_specs=[
            pl.BlockSpec((tm, d), lambda m: (m, 0)),
            pl.BlockSpec((tm, w_a), lambda m: (m, 0)),
            pl.BlockSpec((tm, w_b), lambda m: (m, 0)),
            pl.BlockSpec((tm, w_c), lambda m: (m, 0)),
            pl.BlockSpec((w_a + w_b + w_c, d), lambda m: (0, 0)),
        ],
        out_specs=pl.BlockSpec((tm, d), lambda m: (m, 0)),
        out_shape=jax.ShapeDtypeStruct((rows, d), F32),
        compiler_params=_params(1),
        name="merge",
    )(x, mf, md, ms, w_out_bf16)


def _logf_suffix_body(x_ref, tri_ref, se_ref, si_ref):
    x = x_ref[...]
    si = jnp.dot(x, tri_ref[...], precision=HIGHEST, preferred_element_type=F32)
    si_ref[...] = si
    se_ref[...] = si - x


def _logf_suffix(logf_t):
    rows = logf_t.shape[0]
    tr = max(t for t in range(SUBLANES, 2048 + 1, SUBLANES) if rows % t == 0)
    tri = jnp.tril(jnp.ones((PAGE, PAGE), F32))
    return pl.pallas_call(
        _logf_suffix_body,
        grid=(rows // tr,),
        in_specs=[pl.BlockSpec((tr, PAGE), lambda r: (r, 0)), pl.BlockSpec((PAGE, PAGE), lambda r: (0, 0))],
        out_specs=[pl.BlockSpec((tr, PAGE), lambda r: (r, 0))] * 2,
        out_shape=[jax.ShapeDtypeStruct((rows, PAGE), F32)] * 2,
        compiler_params=_params(1),
        name="logf_suffix",
    )(logf_t, tri)


def _online_update(s, v, m_scr, l_scr, acc_scr, idx):
    m_prev = m_scr[idx]
    m_new = jnp.maximum(m_prev, jnp.max(s, axis=-1, keepdims=True))
    alpha = jnp.exp(m_prev - m_new)
    p = jnp.exp(s - m_new[:, 0:1])
    l_scr[idx] = alpha * l_scr[idx] + jnp.sum(p, axis=-1, keepdims=True)
    pv = jnp.dot(p.astype(BF16), v, preferred_element_type=F32)
    acc_scr[idx] = alpha[:, 0:1] * acc_scr[idx] + pv
    m_scr[idx] = m_new


def _sample_attn_body(pt_ref, *refs, n_q, pps, n_fh, hd_a, n_dh, hd_b, lambda_init):
    del pt_ref
    n_in = 10 + 6 * pps
    (fq_ref, fkn_ref, fvn_ref, fg_ref, dq_ref, dkn_ref, dvn_ref, dg_ref, lfn_ref, sb_ref) = refs[:10]
    page_refs = refs[10:n_in]
    lam_ref, sub_ref, of_ref, od_ref = refs[n_in:n_in + 4]
    (qf_scr, qd_scr, kn_scr, vn_scr, dkn_scr, dvn_scr, cq_scr, r_scr,
     fm_scr, fl_scr, facc_scr, dm_scr, dl_scr, dacc_scr) = refs[n_in + 4:]
    j = pl.program_id(1)
    nj = pl.num_programs(1)
    n_fp = n_fh // 2
    n_dp = n_dh // 2
    fw = 2 * hd_a
    hw = 2 * hd_b
    dw = 2 * hw
    fscale = hd_a ** -0.5
    dscale = hd_b ** -0.5
    frow = lax.broadcasted_iota(jnp.int32, (2 * n_q, PAGE), 0)
    flane = lax.broadcasted_iota(jnp.int32, (2 * n_q, PAGE), 1)

    def fox_scores(p, k_tile):
        k = k_tile[:, p * fw:(p + 1) * fw].astype(BF16)
        return lax.dot_general(qf_scr[p].astype(BF16), k, _NT, preferred_element_type=F32) * fscale

    def diff_scores(p, k_tile):
        k = k_tile[:, p * dw:(p + 1) * dw].astype(BF16)
        return lax.dot_general(qd_scr[p].astype(BF16), k, _NT, preferred_element_type=F32) * dscale

    def pair_rows(ref, p):
        return jnp.where(frow < n_q, ref[2 * p:2 * p + 1, :], ref[2 * p + 1:2 * p + 2, :])

    @pl.when(j == 0)
    def _():
        qf_scr[...] = jnp.zeros_like(qf_scr)
        qd_scr[...] = jnp.zeros_like(qd_scr)
        fq = fq_ref[...]
        dq = dq_ref[...]
        for p in range(n_fp):
            qf_scr[p, 0:n_q, 0:hd_a] = fq[:, 2 * p * hd_a:(2 * p + 1) * hd_a]
            qf_scr[p, n_q:2 * n_q, hd_a:fw] = fq[:, (2 * p + 1) * hd_a:(2 * p + 2) * hd_a]
        for p in range(n_dp):
            for hh in range(2):
                for mp in range(2):
                    r0 = (2 * hh + mp) * n_q
                    c0 = hh * hw + mp * hd_b
                    src = (2 * p + hh) * hw + mp * hd_b
                    qd_scr[p, r0:r0 + n_q, c0:c0 + hd_b] = dq[:, src:src + hd_b]
        kn_scr[...] = jnp.zeros_like(kn_scr)
        vn_scr[...] = jnp.zeros_like(vn_scr)
        dkn_scr[...] = jnp.zeros_like(dkn_scr)
        dvn_scr[...] = jnp.zeros_like(dvn_scr)
        kn_scr[0:n_q, :] = fkn_ref[...]
        vn_scr[0:n_q, :] = fvn_ref[...]
        dkn_scr[0:n_q, :] = dkn_ref[...]
        dvn_scr[0:n_q, :] = dvn_ref[...]
        fm_scr[...] = jnp.full_like(fm_scr, NEG)
        fl_scr[...] = jnp.zeros_like(fl_scr)
        facc_scr[...] = jnp.zeros_like(facc_scr)
        dm_scr[...] = jnp.full_like(dm_scr, NEG)
        dl_scr[...] = jnp.zeros_like(dl_scr)
        dacc_scr[...] = jnp.zeros_like(dacc_scr)
        r_scr[...] = jnp.zeros_like(r_scr)
        lf = lfn_ref[...]
        rows = [lf[0:1, :]]
        for t in range(1, n_q):
            rows.append(rows[-1] + lf[t:t + 1, :])
        for p in range(n_fp):
            for hh in range(2):
                for t in range(n_q):
                    cq_scr[p, hh * n_q + t:hh * n_q + t + 1, :] = jnp.broadcast_to(
                        rows[t][:, 2 * p + hh:2 * p + hh + 1], (1, PAGE))
        for p in range(n_fp):
            cq = cq_scr[p]
            bias = jnp.full((2 * n_q, PAGE), NEG, F32)
            for t in range(n_q):
                ck = jnp.where(frow < n_q, rows[t][:, 2 * p:2 * p + 1], rows[t][:, 2 * p + 1:2 * p + 2])
                ok = (flane == t) & ((frow % n_q) >= t)
                bias = jnp.where(ok, cq - ck, bias)
            s = fox_scores(p, kn_scr) + bias
            _online_update(s, vn_scr[:, p * fw:(p + 1) * fw].astype(BF16), fm_scr, fl_scr, facc_scr, p)
        for p in range(n_dp):
            s = diff_scores(p, dkn_scr) + sb_ref[1, p]
            _online_update(s, dvn_scr[:, p * dw:(p + 1) * dw].astype(BF16), dm_scr, dl_scr, dacc_scr, p)

    def do_pages(diff_kind_first):
        for g in range(pps):
            fk_ref, fv_ref, dk_ref, dv_ref, se_ref, si_ref = page_refs[6 * g:6 * g + 6]
            for p in range(n_fp):
                bias = cq_scr[p] + pair_rows(se_ref, p) + r_scr[p]
                s = fox_scores(p, fk_ref) + bias
                _online_update(s, fv_ref[:, p * fw:(p + 1) * fw].astype(BF16), fm_scr, fl_scr, facc_scr, p)
            for p in range(n_fp):
                tot = jnp.where(frow < n_q, si_ref[2 * p:2 * p + 1, 0:1], si_ref[2 * p + 1:2 * p + 2, 0:1])
                r_scr[p] = r_scr[p] + tot
            kind = diff_kind_first if g == 0 else 2
            for p in range(n_dp):
                s = diff_scores(p, dk_ref) + sb_ref[kind, p]
                _online_update(s, dv_ref[:, p * dw:(p + 1) * dw].astype(BF16), dm_scr, dl_scr, dacc_scr, p)

    @pl.when(j == 0)
    def _():
        do_pages(0)

    @pl.when(j > 0)
    def _():
        do_pages(2)

    @pl.when(j == nj - 1)
    def _():
        for p in range(n_fp):
            o = facc_scr[p] / fl_scr[p][:, 0:1]
            for hh in range(2):
                h = 2 * p + hh
                oh = o[hh * n_q:(hh + 1) * n_q, hh * hd_a:(hh + 1) * hd_a]
                of_ref[:, h * hd_a:(h + 1) * hd_a] = oh * _silu(fg_ref[:, h * hd_a:(h + 1) * hd_a])
        lam = _diff_lambda(lam_ref, lambda_init)
        for p in range(n_dp):
            o = dacc_scr[p] / dl_scr[p][:, 0:1]
            for hh in range(2):
                h = 2 * p + hh
                r0 = 2 * hh * n_q
                o0 = o[r0:r0 + n_q, hh * hw:(hh + 1) * hw]
                o1 = o[r0 + n_q:r0 + 2 * n_q, hh * hw:(hh + 1) * hw]
                od = o0 - lam * o1
                ms = jnp.mean(od * od, axis=-1, keepdims=True)
                od = od * lax.rsqrt(ms + EPS) * sub_ref[...] * (1.0 - lambda_init)
                od_ref[:, h * hw:(h + 1) * hw] = od * _silu(dg_ref[:, h * hw:(h + 1) * hw])


def _sample_attn(page_table, z3, logf3, sbias, caches, lam_vecs, sub_gain, *, layer, n_fh, hd_a, n_dh, hd_b,
                 lambda_init, col_d):
    fk_c, fv_c, dk_c, dv_c, se_c, si_c = caches
    n_seq, n_q, _ = z3.shape
    n_pages = page_table.shape[1]
    pps = 2 if n_pages % 2 == 0 else 1
    nj = n_pages // pps
    w_a = n_fh * hd_a
    w_b = n_dh * 2 * hd_b
    cb = col_d // w_b
    body = functools.partial(_sample_attn_body, n_q=n_q, pps=pps, n_fh=n_fh, hd_a=hd_a, n_dh=n_dh, hd_b=hd_b,
                             lambda_init=lambda_init)
    zspec = lambda w, c: pl.BlockSpec((None, n_q, w), lambda s, j, pt: (s, 0, c))

    def page_specs(g):
        idx = lambda s, j, pt: (layer, pt[s, n_pages - 1 - (j * pps + g)], 0, 0)
        return [
            pl.BlockSpec((None, None, PAGE, w_a), idx), pl.BlockSpec((None, None, PAGE, w_a), idx),
            pl.BlockSpec((None, None, PAGE, w_b), idx), pl.BlockSpec((None, None, PAGE, w_b), idx),
            pl.BlockSpec((None, None, n_fh, PAGE), idx), pl.BlockSpec((None, None, n_fh, PAGE), idx),
        ]

    in_specs = [
        zspec(w_a, 0), zspec(w_a, 1), zspec(w_a, 2), zspec(w_a, 3),
        zspec(w_b, cb), zspec(w_b, cb + 1), zspec(w_b, cb + 2), zspec(w_b, cb + 3),
        pl.BlockSpec((None, n_q, LANES), lambda s, j, pt: (s, 0, 0)),
        pl.BlockSpec(sbias.shape, lambda s, j, pt: (0, 0, 0, 0)),
    ]
    args = [z3] * 8 + [logf3, sbias]
    for g in range(pps):
        in_specs += page_specs(g)
        args += [fk_c, fv_c, dk_c, dv_c, se_c, si_c]
    in_specs += [pl.BlockSpec((4, hd_b), lambda s, j, pt: (0, 0)), pl.BlockSpec((1, 2 * hd_b), lambda s, j, pt: (0, 0))]
    args += [lam_vecs, sub_gain]
    n_fp, n_dp = n_fh // 2, n_dh // 2
    grid_spec = pltpu.PrefetchScalarGridSpec(
        num_scalar_prefetch=1,
        grid=(n_seq, nj),
        in_specs=in_specs,
        out_specs=[
            pl.BlockSpec((None, n_q, w_a), lambda s, j, pt: (s, 0, 0)),
            pl.BlockSpec((None, n_q, w_b), lambda s, j, pt: (s, 0, 0)),
        ],
        scratch_shapes=[
            pltpu.VMEM((n_fp, 2 * n_q, 2 * hd_a), F32),
            pltpu.VMEM((n_dp, 4 * n_q, 4 * hd_b), F32),
            pltpu.VMEM((PAGE, w_a), F32), pltpu.VMEM((PAGE, w_a), F32),
            pltpu.VMEM((PAGE, w_b), F32), pltpu.VMEM((PAGE, w_b), F32),
            pltpu.VMEM((n_fp, 2 * n_q, PAGE), F32), pltpu.VMEM((n_fp, 2 * n_q, PAGE), F32),
            pltpu.VMEM((n_fp, 2 * n_q, LANES), F32), pltpu.VMEM((n_fp, 2 * n_q, LANES), F32),
            pltpu.VMEM((n_fp, 2 * n_q, 2 * hd_a), F32),
            pltpu.VMEM((n_dp, 4 * n_q, LANES), F32), pltpu.VMEM((n_dp, 4 * n_q, LANES), F32),
            pltpu.VMEM((n_dp, 4 * n_q, 4 * hd_b), F32),
        ],
    )
    return pl.pallas_call(
        body,
        grid_spec=grid_spec,
        out_shape=[
            jax.ShapeDtypeStruct((n_seq, n_q, w_a), F32),
            jax.ShapeDtypeStruct((n_seq, n_q, w_b), F32),
        ],
        compiler_params=_params(2),
        name="sample_attn",
    )(page_table, *args)


def kernel(x_prompt, x_sample, cache_fox_k, cache_fox_v, cache_fox_logf, cache_diff_k, cache_diff_v, state_s5_re, state_s5_im, page_table, norm_gain, w_in, fox_b_f, fox_q_gain, fox_k_gain, diff_q_gain, diff_k_gain, diff_lambda_q1, diff_lambda_k1, diff_lambda_q2, diff_lambda_k2, diff_subln_gain, rel_bias, s5_lambda_re, s5_lambda_im, s5_b_re, s5_b_im, s5_c_re, s5_c_im, s5_d, s5_log_step, s5_w_glu, s5_b_glu, w_out):
    batch, seq_len, d_model = x_prompt.shape
    n_seq, n_q, _ = x_sample.shape
    depth, n_pool, page, n_fh, hd_a = cache_fox_k.shape
    n_dh = cache_diff_k.shape[3]
    hd_b = cache_diff_k.shape[4] // 2
    g_c, p_state, c_group = s5_b_re.shape[1:]
    assert page == PAGE and n_fh == SUBLANES and hd_a == LANES and 2 * hd_b == LANES
    w_a = n_fh * hd_a
    w_b = n_dh * 2 * hd_b
    w_c = g_c * c_group
    n_state = g_c * p_state
    col_d = 4 * w_a
    col_s = col_d + 4 * w_b

    xp = x_prompt.reshape(batch * seq_len, d_model)
    xs = x_sample.reshape(n_seq * n_q, d_model)

    fk_c = cache_fox_k.reshape(depth, n_pool, PAGE, w_a)
    fv_c = cache_fox_v.reshape(depth, n_pool, PAGE, w_a)
    dk_c = cache_diff_k.reshape(depth, n_pool, PAGE, w_b)
    dv_c = cache_diff_v.reshape(depth, n_pool, PAGE, w_b)
    logf_t = jnp.swapaxes(cache_fox_logf, 2, 3).reshape(depth * n_pool * n_fh, PAGE)
    se_c, si_c = _logf_suffix(logf_t)
    se_c = se_c.reshape(depth, n_pool, n_fh, PAGE)
    si_c = si_c.reshape(depth, n_pool, n_fh, PAGE)

    tile = min(512, seq_len)
    bias_tiles = _bias_tiles(rel_bias, n_heads=n_dh, tq=tile, tk=tile)
    sbias = _sample_bias(rel_bias, n_heads=n_dh, n_q=n_q)
    tri = jnp.tril(jnp.ones((CUMSUM_CHUNK, CUMSUM_CHUNK), F32))

    st_p, st_s = [], []
    for l in range(depth):
        lambda_init = 0.8 - 0.6 * math.exp(-0.3 * l)
        w = w_in[l]
        w_main = jnp.concatenate([w[:, :3 * w_a], w[:, 3 * w_a + n_fh:]], axis=1).astype(BF16)
        w_f = jnp.pad(w[:, 3 * w_a:3 * w_a + n_fh], ((0, 0), (0, LANES - n_fh))).astype(BF16)
        b_f = jnp.pad(fox_b_f[l], (0, LANES - n_fh)).reshape(1, LANES)
        gains = jnp.stack([fox_q_gain[l], fox_k_gain[l], jnp.tile(diff_q_gain[l], 2), jnp.tile(diff_k_gain[l], 2)])
        g_row = norm_gain[l].reshape(1, d_model)
        w_out_b = w_out[l].astype(BF16)
        lam_vecs = jnp.stack([diff_lambda_q1[l], diff_lambda_k1[l], diff_lambda_q2[l], diff_lambda_k2[l]])
        sub_gain = diff_subln_gain[l].reshape(1, 2 * hd_b)
        are, aim, bbre, bbim = _s5_prep(s5_lambda_re[l], s5_lambda_im[l], s5_log_step[l], s5_b_re[l], s5_b_im[l])
        mats = (
            _block_diag(jnp.swapaxes(bbre, 1, 2)), _block_diag(jnp.swapaxes(bbim, 1, 2)),
            _block_diag(jnp.swapaxes(s5_c_re[l], 1, 2)), _block_diag(jnp.swapaxes(s5_c_im[l], 1, 2)),
            are.reshape(1, n_state), aim.reshape(1, n_state),
        )
        d_vec = s5_d[l].reshape(1, w_c)
        b_glu = s5_b_glu[l].reshape(1, w_c)

        z, logf, c, ct = _proj(xp, g_row, w_main, w_f, b_f, gains, tri, seq_len=seq_len, w_a=w_a, w_b=w_b)
        mf = _fox_prompt(z, c, ct, batch=batch, seq_len=seq_len, n_heads=n_fh, hd=hd_a)
        md = _diff_prompt(z, bias_tiles, lam_vecs, sub_gain, batch=batch, seq_len=seq_len, n_heads=n_dh, hd=hd_b,
                          col0=col_d, lambda_init=lambda_init)
        ms, hre, him = _s5_prompt(z, mats, d_vec, s5_w_glu[l], b_glu, batch=batch, seq_len=seq_len, col0=col_s,
                                  w_c=w_c)
        xp_new = _merge(xp, mf, md, ms, w_out_b)
        st_p.append((
            z[:, w_a:2 * w_a].reshape(batch, seq_len, n_fh, hd_a),
            z[:, 2 * w_a:3 * w_a].reshape(batch, seq_len, n_fh, hd_a),
            logf[:, :n_fh].reshape(batch, seq_len, n_fh),
            z[:, col_d + w_b:col_d + 2 * w_b].reshape(batch, seq_len, n_dh, 2 * hd_b),
            z[:, col_d + 2 * w_b:col_d + 3 * w_b].reshape(batch, seq_len, n_dh, 2 * hd_b),
            hre.reshape(batch, g_c, p_state), him.reshape(batch, g_c, p_state),
        ))

        zs, logf_s, _, _ = _proj(xs, g_row, w_main, w_f, b_f, gains, tri, seq_len=n_seq * n_q, w_a=w_a, w_b=w_b)
        z3 = zs.reshape(n_seq, n_q, zs.shape[1])
        mf_s, md_s = _sample_attn(page_table, z3, logf_s.reshape(n_seq, n_q, LANES), sbias,
                                  (fk_c, fv_c, dk_c, dv_c, se_c, si_c), lam_vecs, sub_gain, layer=l, n_fh=n_fh,
                                  hd_a=hd_a, n_dh=n_dh, hd_b=hd_b, lambda_init=lambda_init, col_d=col_d)
        u_s = zs[:, col_s:col_s + w_c].reshape(n_seq, n_q * w_c)
        g_s = zs[:, col_s + w_c:col_s + 2 * w_c].reshape(n_seq, n_q * w_c)
        ms_s, hre_s, him_s = _s5_sample(u_s, g_s, state_s5_re[l].reshape(n_seq, n_state),
                                        state_s5_im[l].reshape(n_seq, n_state), mats, d_vec, s5_w_glu[l], b_glu,
                                        n_steps=n_q, w_c=w_c)
        xs_new = _merge(xs, mf_s.reshape(n_seq * n_q, w_a), md_s.reshape(n_seq * n_q, w_b),
                        ms_s.reshape(n_seq * n_q, w_c), w_out_b)
        st_s.append((
            zs[:, w_a:2 * w_a].reshape(n_seq, n_q, n_fh, hd_a),
            zs[:, 2 * w_a:3 * w_a].reshape(n_seq, n_q, n_fh, hd_a),
            logf_s[:, :n_fh].reshape(n_seq, n_q, n_fh),
            zs[:, col_d + w_b:col_d + 2 * w_b].reshape(n_seq, n_q, n_dh, 2 * hd_b),
            zs[:, col_d + 2 * w_b:col_d + 3 * w_b].reshape(n_seq, n_q, n_dh, 2 * hd_b),
            hre_s.reshape(n_seq, g_c, p_state), him_s.reshape(n_seq, g_c, p_state),
        ))
        xp, xs = xp_new, xs_new

    outs_p = [jnp.stack([s[i] for s in st_p]) for i in range(7)]
    outs_s = [jnp.stack([s[i] for s in st_s]) for i in range(7)]
    return (xp.reshape(batch, seq_len, d_model), xs.reshape(n_seq, n_q, d_model), *outs_p, *outs_s)
```

```python
import functools
import math

import jax
import jax.numpy as jnp
from jax import lax
from jax.experimental import pallas as pl
from jax.experimental.pallas import tpu as pltpu

F32 = jnp.float32
BF16 = jnp.bfloat16
HIGHEST = lax.Precision.HIGHEST

EPS = 1e-6
NEG = -1e30
LANES = 128
SUBLANES = 8
PAGE = 128
N_BUCKETS = 32
REL_MAX_DIST = 128
VMEM_LIMIT = 56 * 1024 * 1024

_NT = (((1,), (1,)), ((), ()))


def _params(n_axes, vmem=VMEM_LIMIT):
    return pltpu.CompilerParams(dimension_semantics=("arbitrary",) * n_axes, vmem_limit_bytes=vmem)


def _silu(x):
    return x / (1.0 + jnp.exp(-x))


def _sigmoid(x):
    return 1.0 / (1.0 + jnp.exp(-x))


def _log_sigmoid(x):
    return jnp.minimum(x, 0.0) - jnp.log1p(jnp.exp(-jnp.abs(x)))


def _gelu_tanh(x):
    return 0.5 * x * (1.0 + jnp.tanh(math.sqrt(2.0 / math.pi) * (x + 0.044715 * (x * x * x))))


CUMSUM_CHUNK = 256


def _proj_body(x_ref, g_ref, w_ref, wf_ref, bf_ref, gains_ref, tri_ref,
               z_ref, logf_ref, c_ref, ct_ref, h_scr, carry_scr, *, tm, tn, seq_tiles, seg):
    m = pl.program_id(0)
    n = pl.program_id(1)

    @pl.when(n == 0)
    def _():
        x = x_ref[...]
        ms = jnp.mean(x * x, axis=-1, keepdims=True)
        hb = (x * lax.rsqrt(ms + EPS) * g_ref[...]).astype(BF16)
        h_scr[...] = hb
        logits = jnp.dot(hb, wf_ref[...], preferred_element_type=F32) + bf_ref[...]
        lf = _log_sigmoid(logits)
        logf_ref[...] = lf

        @pl.when(m % seq_tiles == 0)
        def _():
            carry_scr[...] = jnp.zeros_like(carry_scr)

        carry = carry_scr[0:1, :]
        for j in range(tm // CUMSUM_CHUNK):
            blk = lf[j * CUMSUM_CHUNK:(j + 1) * CUMSUM_CHUNK]
            cs = jnp.dot(tri_ref[...], blk, precision=HIGHEST, preferred_element_type=F32) + carry
            c_ref[j * CUMSUM_CHUNK:(j + 1) * CUMSUM_CHUNK, :] = cs
            carry = cs[CUMSUM_CHUNK - 1:CUMSUM_CHUNK, :]
        carry_scr[...] = jnp.broadcast_to(carry, carry_scr.shape)
        ct_ref[...] = c_ref[...].T[0:SUBLANES, :]

    acc = jnp.dot(h_scr[...], w_ref[...], preferred_element_type=F32)
    lane = lax.broadcasted_iota(jnp.int32, (tm, LANES), 1)

    def norm_full(row):
        gain = gains_ref[row:row + 1, :]
        for j in range(tn // LANES):
            y = acc[:, j * LANES:(j + 1) * LANES]
            ms = jnp.mean(y * y, axis=-1, keepdims=True)
            z_ref[:, j * LANES:(j + 1) * LANES] = y * lax.rsqrt(ms + EPS) * gain

    def norm_half(row):
        gain = gains_ref[row:row + 1, :]
        half = LANES // 2
        for j in range(tn // LANES):
            y = acc[:, j * LANES:(j + 1) * LANES]
            y2 = y * y
            lo = jnp.sum(jnp.where(lane < half, y2, 0.0), axis=-1, keepdims=True)
            hi = jnp.sum(jnp.where(lane >= half, y2, 0.0), axis=-1, keepdims=True)
            ms = jnp.where(lane < half, lo, hi) * (1.0 / half)
            z_ref[:, j * LANES:(j + 1) * LANES] = y * lax.rsqrt(ms + EPS) * gain

    fq0, fk0, fv0, dq0, dk0, dv0 = seg

    @pl.when(n < fk0)
    def _():
        norm_full(0)

    @pl.when((n >= fk0) & (n < fv0))
    def _():
        norm_full(1)

    @pl.when(n == dq0)
    def _():
        norm_half(2)

    @pl.when(n == dk0)
    def _():
        norm_half(3)

    @pl.when(((n >= fv0) & (n < dq0)) | (n >= dv0))
    def _():
        z_ref[...] = acc


def _proj(x, g, w_main, w_f, b_f, gains, tri, *, seq_len, w_a, w_b):
    rows, d = x.shape
    n_cols = w_main.shape[1]
    tm = min(1024, rows)
    tn = 512
    assert rows % tm == 0 and n_cols % tn == 0 and seq_len % tm == 0 and tm % CUMSUM_CHUNK == 0
    assert w_a % tn == 0 and w_b == tn
    seg = (0, w_a // tn, 2 * w_a // tn, 4 * w_a // tn, 4 * w_a // tn + 1, 4 * w_a // tn + 2)
    body = functools.partial(_proj_body, tm=tm, tn=tn, seq_tiles=seq_len // tm, seg=seg)
    return pl.pallas_call(
        body,
        grid=(rows // tm, n_cols // tn),
        in_specs=[
            pl.BlockSpec((tm, d), lambda m, n: (m, 0)),
            pl.BlockSpec((1, d), lambda m, n: (0, 0)),
            pl.BlockSpec((d, tn), lambda m, n: (0, n)),
            pl.BlockSpec((d, LANES), lambda m, n: (0, 0)),
            pl.BlockSpec((1, LANES), lambda m, n: (0, 0)),
            pl.BlockSpec((4, LANES), lambda m, n: (0, 0)),
            pl.BlockSpec((CUMSUM_CHUNK, CUMSUM_CHUNK), lambda m, n: (0, 0)),
        ],
        out_specs=[
            pl.BlockSpec((tm, tn), lambda m, n: (m, n)),
            pl.BlockSpec((tm, LANES), lambda m, n: (m, 0)),
            pl.BlockSpec((tm, LANES), lambda m, n: (m, 0)),
            pl.BlockSpec((SUBLANES, tm), lambda m, n: (0, m)),
        ],
        out_shape=[
            jax.ShapeDtypeStruct((rows, n_cols), F32),
            jax.ShapeDtypeStruct((rows, LANES), F32),
            jax.ShapeDtypeStruct((rows, LANES), F32),
            jax.ShapeDtypeStruct((SUBLANES, rows), F32),
        ],
        scratch_shapes=[pltpu.VMEM((tm, d), BF16), pltpu.VMEM((SUBLANES, LANES), F32)],
        compiler_params=_params(2),
        name="proj",
    )(x, g, w_main, w_f, b_f, gains, tri)


def _fox_body(q_ref, k_ref, v_ref, c_ref, ct_ref, gate_ref, o_ref, m_scr, l_scr, acc_scr, *, n_heads, hd, tq, tk):
    qi = pl.program_id(1)
    ki = pl.program_id(2)
    nk = pl.num_programs(2)
    scale = hd ** -0.5

    @pl.when(ki == 0)
    def _():
        m_scr[...] = jnp.full_like(m_scr, NEG)
        l_scr[...] = jnp.zeros_like(l_scr)
        acc_scr[...] = jnp.zeros_like(acc_scr)

    @pl.when(ki <= qi)
    def _():
        row = lax.broadcasted_iota(jnp.int32, (tq, tk), 0) + qi * tq
        col = lax.broadcasted_iota(jnp.int32, (tq, tk), 1) + ki * tk
        visible = col <= row
        for h in range(n_heads):
            sl = slice(h * hd, (h + 1) * hd)
            q = q_ref[:, sl].astype(BF16)
            k = k_ref[:, sl].astype(BF16)
            s = lax.dot_general(q, k, _NT, preferred_element_type=F32) * scale
            s = s + c_ref[:, h:h + 1] - ct_ref[h:h + 1, :]
            s = jnp.where(visible, s, NEG)
            m_prev = m_scr[h]
            m_new = jnp.maximum(m_prev, jnp.max(s, axis=-1, keepdims=True))
            alpha = jnp.exp(m_prev - m_new)
            p = jnp.exp(s - m_new[:, 0:1])
            l_scr[h] = alpha * l_scr[h] + jnp.sum(p, axis=-1, keepdims=True)
            pv = jnp.dot(p.astype(BF16), v_ref[:, sl].astype(BF16), preferred_element_type=F32)
            acc_scr[:, sl] = alpha * acc_scr[:, sl] + pv
            m_scr[h] = m_new

    @pl.when(ki == nk - 1)
    def _():
        for h in range(n_heads):
            sl = slice(h * hd, (h + 1) * hd)
            o = acc_scr[:, sl] / l_scr[h]
            o_ref[:, sl] = (o * _silu(gate_ref[:, sl])).astype(BF16)


def _fox_prompt(z, c, ct, *, batch, seq_len, n_heads, hd):
    w = n_heads * hd
    tq = tk = min(512, seq_len)
    nq = seq_len // tq
    body = functools.partial(_fox_body, n_heads=n_heads, hd=hd, tq=tq, tk=tk)
    kv_row = lambda b, qi, ki: b * nq + jnp.minimum(ki, qi)
    return pl.pallas_call(
        body,
        grid=(batch, nq, nq),
        in_specs=[
            pl.BlockSpec((tq, w), lambda b, qi, ki: (b * nq + qi, 0)),
            pl.BlockSpec((tk, w), lambda b, qi, ki: (kv_row(b, qi, ki), 1)),
            pl.BlockSpec((tk, w), lambda b, qi, ki: (kv_row(b, qi, ki), 2)),
            pl.BlockSpec((tq, LANES), lambda b, qi, ki: (b * nq + qi, 0)),
            pl.BlockSpec((SUBLANES, tk), lambda b, qi, ki: (0, kv_row(b, qi, ki))),
            pl.BlockSpec((tq, w), lambda b, qi, ki: (b * nq + qi, 3)),
        ],
        out_specs=pl.BlockSpec((tq, w), lambda b, qi, ki: (b * nq + qi, 0)),
        out_shape=jax.ShapeDtypeStruct((batch * seq_len, w), BF16),
        scratch_shapes=[
            pltpu.VMEM((n_heads, tq, LANES), F32),
            pltpu.VMEM((n_heads, tq, LANES), F32),
            pltpu.VMEM((tq, w), F32),
        ],
        compiler_params=_params(3),
        name="fox_prompt",
    )(z, z, z, c, ct, z)


def _rel_bucket(d):
    max_exact = N_BUCKETS // 2
    df = jnp.maximum(d, 1).astype(F32)
    large = max_exact + (jnp.log(df / max_exact) / math.log(REL_MAX_DIST / max_exact)
                         * (N_BUCKETS - max_exact)).astype(jnp.int32)
    large = jnp.minimum(large, N_BUCKETS - 1)
    return jnp.where(d < max_exact, d, large)


def _bias_lookup(rel_ref, d, h):
    bucket = _rel_bucket(jnp.maximum(d, 0))
    val = jnp.full(d.shape, rel_ref[0, h], F32)
    for b in range(1, N_BUCKETS):
        val = jnp.where(bucket == b, rel_ref[b, h], val)
    return jnp.where(d >= 0, val, NEG)


def _bias_tiles_body(rel_ref, o_ref, *, tq, tk):
    h = pl.program_id(0)
    kind = pl.program_id(1)
    row = lax.broadcasted_iota(jnp.int32, (tq, tk), 0)
    col = lax.broadcasted_iota(jnp.int32, (tq, tk), 1)
    o_ref[...] = _bias_lookup(rel_ref, kind * tq + row - col, h)


def _bias_tiles(rel_bias, *, n_heads, tq, tk):
    return pl.pallas_call(
        functools.partial(_bias_tiles_body, tq=tq, tk=tk),
        grid=(n_heads, 3),
        in_specs=[pl.BlockSpec(memory_space=pltpu.SMEM)],
        out_specs=pl.BlockSpec((None, None, tq, tk), lambda h, kind: (h, kind, 0, 0)),
        out_shape=jax.ShapeDtypeStruct((n_heads, 3, tq, tk), F32),
        compiler_params=_params(2),
        name="bias_tiles",
    )(rel_bias)


def _sample_bias_body(rel_ref, o_ref, *, n_q, n_heads):
    kind = pl.program_id(0)
    rows = 2 * n_q * n_heads
    width = PAGE * n_heads
    r = lax.broadcasted_iota(jnp.int32, (rows, width), 0)
    lane = lax.broadcasted_iota(jnp.int32, (rows, width), 1)
    i = (r % (n_q * n_heads)) // n_heads
    pos = lane // n_heads
    d_last = PAGE + i - pos
    d_new = jnp.where(pos < n_q, i - pos, -1)
    d_far = jnp.full((rows, width), 2 * PAGE, jnp.int32)
    d = jnp.where(kind == 0, d_last, jnp.where(kind == 1, d_new, d_far))
    out = jnp.full((rows, width), NEG, F32)
    for h in range(n_heads):
        own = (lane % n_heads == h) & (r % n_heads == h)
        out = jnp.where(own, _bias_lookup(rel_ref, d, h), out)
    o_ref[...] = out


def _sample_bias(rel_bias, *, n_heads, n_q):
    rows = 2 * n_q * n_heads
    width = PAGE * n_heads
    return pl.pallas_call(
        functools.partial(_sample_bias_body, n_q=n_q, n_heads=n_heads),
        grid=(3,),
        in_specs=[pl.BlockSpec(memory_space=pltpu.SMEM)],
        out_specs=pl.BlockSpec((None, rows, width), lambda kind: (kind, 0, 0)),
        out_shape=jax.ShapeDtypeStruct((3, rows, width), F32),
        compiler_params=_params(1),
        name="sample_bias",
    )(rel_bias)


def _diff_lambda(lam_ref, lambda_init):
    lv = lam_ref[...]
    e1 = jnp.exp(jnp.sum(lv[0:1, :] * lv[1:2, :], axis=-1, keepdims=True))
    e2 = jnp.exp(jnp.sum(lv[2:3, :] * lv[3:4, :], axis=-1, keepdims=True))
    return e1 - e2 + lambda_init


def _diff_body(q_ref, k_ref, v_ref, bias_ref, gate_ref, lam_ref, sub_ref, o_ref,
               m_scr, l_scr, acc_scr, *, n_heads, hd, tq, tk, lambda_init):
    qi = pl.program_id(1)
    ki = pl.program_id(2)
    nk = pl.num_programs(2)
    hw = 2 * hd
    scale = hd ** -0.5

    @pl.when(ki == 0)
    def _():
        m_scr[...] = jnp.full_like(m_scr, NEG)
        l_scr[...] = jnp.zeros_like(l_scr)
        acc_scr[...] = jnp.zeros_like(acc_scr)

    @pl.when(ki <= qi)
    def _():
        lane = lax.broadcasted_iota(jnp.int32, (tq, hw), 1)
        for h in range(n_heads):
            sl = slice(h * hw, (h + 1) * hw)
            q = q_ref[:, sl]
            k = k_ref[:, sl].astype(BF16)
            v = v_ref[:, sl].astype(BF16)
            bias = bias_ref[h]
            for mp in range(2):
                qm = jnp.where((lane < hd) if mp == 0 else (lane >= hd), q, 0.0).astype(BF16)
                s = lax.dot_general(qm, k, _NT, preferred_element_type=F32) * scale + bias
                idx = 2 * h + mp
                m_prev = m_scr[idx]
                m_new = jnp.maximum(m_prev, jnp.max(s, axis=-1, keepdims=True))
                alpha = jnp.exp(m_prev - m_new)
                p = jnp.exp(s - m_new[:, 0:1])
                l_scr[idx] = alpha * l_scr[idx] + jnp.sum(p, axis=-1, keepdims=True)
                pv = jnp.dot(p.astype(BF16), v, preferred_element_type=F32)
                acc_scr[idx] = alpha * acc_scr[idx] + pv
                m_scr[idx] = m_new

    @pl.when(ki == nk - 1)
    def _():
        lam = _diff_lambda(lam_ref, lambda_init)
        for h in range(n_heads):
            sl = slice(h * hw, (h + 1) * hw)
            o = acc_scr[2 * h] / l_scr[2 * h] - lam * (acc_scr[2 * h + 1] / l_scr[2 * h + 1])
            ms = jnp.mean(o * o, axis=-1, keepdims=True)
            o = o * lax.rsqrt(ms + EPS) * sub_ref[...] * (1.0 - lambda_init)
            o_ref[:, sl] = (o * _silu(gate_ref[:, sl])).astype(BF16)


def _diff_prompt(z, bias_tiles, lam_vecs, sub_gain, *, batch, seq_len, n_heads, hd, col0, lambda_init):
    hw = 2 * hd
    w = n_heads * hw
    tq = tk = min(512, seq_len)
    nq = seq_len // tq
    cb = col0 // w
    body = functools.partial(_diff_body, n_heads=n_heads, hd=hd, tq=tq, tk=tk, lambda_init=lambda_init)
    kv_row = lambda b, qi, ki: b * nq + jnp.minimum(ki, qi)
    kind = lambda b, qi, ki: jnp.clip(qi - jnp.minimum(ki, qi), 0, 2)
    return pl.pallas_call(
        body,
        grid=(batch, nq, nq),
        in_specs=[
            pl.BlockSpec((tq, w), lambda b, qi, ki: (b * nq + qi, cb)),
            pl.BlockSpec((tk, w), lambda b, qi, ki: (kv_row(b, qi, ki), cb + 1)),
            pl.BlockSpec((tk, w), lambda b, qi, ki: (kv_row(b, qi, ki), cb + 2)),
            pl.BlockSpec((n_heads, None, tq, tk), lambda b, qi, ki: (0, kind(b, qi, ki), 0, 0)),
            pl.BlockSpec((tq, w), lambda b, qi, ki: (b * nq + qi, cb + 3)),
            pl.BlockSpec((4, hd), lambda b, qi, ki: (0, 0)),
            pl.BlockSpec((1, hw), lambda b, qi, ki: (0, 0)),
        ],
        out_specs=pl.BlockSpec((tq, w), lambda b, qi, ki: (b * nq + qi, 0)),
        out_shape=jax.ShapeDtypeStruct((batch * seq_len, w), BF16),
        scratch_shapes=[
            pltpu.VMEM((2 * n_heads, tq, LANES), F32),
            pltpu.VMEM((2 * n_heads, tq, LANES), F32),
            pltpu.VMEM((2 * n_heads, tq, hw), F32),
        ],
        compiler_params=_params(3),
        name="diff_prompt",
    )(z, z, z, bias_tiles, z, lam_vecs, sub_gain)


def _s5_prep_body(lre_ref, lim_ref, ls_ref, bre_ref, bim_ref, are_ref, aim_ref, bbre_ref, bbim_ref):
    lr = lre_ref[...]
    li = lim_ref[...]
    step = jnp.exp(ls_ref[...])
    mag = jnp.exp(lr * step)
    ar = mag * jnp.cos(li * step)
    ai = mag * jnp.sin(li * step)
    den = lr * lr + li * li
    nr = ar - 1.0
    cr = (nr * lr + ai * li) / den
    ci = (ai * lr - nr * li) / den
    br = bre_ref[...]
    bi = bim_ref[...]
    are_ref[...] = ar
    aim_ref[...] = ai
    bbre_ref[...] = cr * br - ci * bi
    bbim_ref[...] = cr * bi + ci * br


def _s5_prep(lam_re, lam_im, log_step, b_re, b_im):
    g, p, c = b_re.shape
    col = lambda a: a.reshape(g, p, 1)
    ls = jnp.broadcast_to(log_step.reshape(g, 1, 1), (g, p, 1))
    full3 = lambda shp: pl.BlockSpec(shp, lambda: (0, 0, 0))
    return pl.pallas_call(
        _s5_prep_body,
        in_specs=[full3((g, p, 1))] * 3 + [full3((g, p, c))] * 2,
        out_specs=[full3((g, p, 1))] * 2 + [full3((g, p, c))] * 2,
        out_shape=[jax.ShapeDtypeStruct((g, p, 1), F32)] * 2 + [jax.ShapeDtypeStruct((g, p, c), F32)] * 2,
        name="s5_prep",
    )(col(lam_re), col(lam_im), ls, b_re, b_im)


def _block_diag(blocks):
    g, r, c = blocks.shape
    eye = jnp.eye(g, dtype=blocks.dtype)
    return (blocks[:, :, None, :] * eye[:, None, :, None]).reshape(g * r, g * c)


S5_LANE_CHUNK = 256


def _s5_scan_body(u_ref, gate_ref, bre_ref, bim_ref, cre_ref, cim_ref, are_ref, aim_ref, d_ref, wg_ref, bg_ref,
                  o_ref, hre_ref, him_ref, pwr_scr, pwi_scr, hr_scr, hi_scr, cr_scr, ci_scr, *, tm, n_state):
    b = pl.program_id(0)
    i = pl.program_id(1)
    nt = pl.num_programs(1)
    n_levels = tm.bit_length() - 1

    @pl.when((b == 0) & (i == 0))
    def _():
        pwr_scr[0:1, :] = are_ref[...]
        pwi_scr[0:1, :] = aim_ref[...]
        for lv in range(n_levels):
            s = 1 << lv
            tr = pwr_scr[s - 1:s, :]
            ti = pwi_scr[s - 1:s, :]
            xr = pwr_scr[0:s, :]
            xi = pwi_scr[0:s, :]
            pwr_scr[s:2 * s, :] = xr * tr - xi * ti
            pwi_scr[s:2 * s, :] = xr * ti + xi * tr

    @pl.when(i == 0)
    def _():
        cr_scr[...] = jnp.zeros_like(cr_scr)
        ci_scr[...] = jnp.zeros_like(ci_scr)

    u = u_ref[...]
    ub = u.astype(BF16)
    hr_scr[...] = jnp.dot(ub, bre_ref[...], preferred_element_type=F32)
    hi_scr[...] = jnp.dot(ub, bim_ref[...], preferred_element_type=F32)

    row = lax.broadcasted_iota(jnp.int32, (tm, S5_LANE_CHUNK), 0)

    def chunk(j, carry):
        ln = pl.ds(pl.multiple_of(j * S5_LANE_CHUNK, S5_LANE_CHUNK), S5_LANE_CHUNK)
        hr = hr_scr[:, ln]
        hi = hi_scr[:, ln]
        for lv in range(n_levels):
            s = 1 << lv
            mr = pwr_scr[s - 1:s, ln]
            mi = pwi_scr[s - 1:s, ln]
            sr = jnp.where(row >= s, pltpu.roll(hr, s, 0), 0.0)
            si = jnp.where(row >= s, pltpu.roll(hi, s, 0), 0.0)
            hr, hi = hr + (mr * sr - mi * si), hi + (mr * si + mi * sr)
        pr = cr_scr[0:1, ln]
        pi = ci_scr[0:1, ln]
        wr = pwr_scr[:, ln]
        wi = pwi_scr[:, ln]
        hr, hi = hr + (wr * pr - wi * pi), hi + (wr * pi + wi * pr)
        hr_scr[:, ln] = hr
        hi_scr[:, ln] = hi
        cr_scr[:, ln] = jnp.broadcast_to(hr[tm - 1:tm, :], (SUBLANES, S5_LANE_CHUNK))
        ci_scr[:, ln] = jnp.broadcast_to(hi[tm - 1:tm, :], (SUBLANES, S5_LANE_CHUNK))
        return carry

    lax.fori_loop(0, n_state // S5_LANE_CHUNK, chunk, 0)

    y = (jnp.dot(hr_scr[...].astype(BF16), cre_ref[...], preferred_element_type=F32)
         - jnp.dot(hi_scr[...].astype(BF16), cim_ref[...], preferred_element_type=F32)
         + d_ref[...] * u)
    zz = _gelu_tanh(y)
    glu = zz * _sigmoid(jnp.dot(zz.astype(BF16), wg_ref[...], preferred_element_type=F32) + bg_ref[...])
    o_ref[...] = (glu * _silu(gate_ref[...])).astype(BF16)

    @pl.when(i == nt - 1)
    def _():
        hre_ref[...] = cr_scr[0:1, :]
        him_ref[...] = ci_scr[0:1, :]


def _s5_prompt(z, mats, d_vec, w_glu, b_glu, *, batch, seq_len, col0, w_c):
    bre, bim, cre, cim, are, aim = mats
    n_state = bre.shape[1]
    tm = min(512, seq_len)
    nt = seq_len // tm
    cb = col0 // w_c
    body = functools.partial(_s5_scan_body, tm=tm, n_state=n_state)
    const = lambda shp: pl.BlockSpec(shp, lambda b, i: (0,) * len(shp))
    return pl.pallas_call(
        body,
        grid=(batch, nt),
        in_specs=[
            pl.BlockSpec((tm, w_c), lambda b, i: (b * nt + i, cb)),
            pl.BlockSpec((tm, w_c), lambda b, i: (b * nt + i, cb + 1)),
            const((w_c, n_state)), const((w_c, n_state)),
            const((n_state, w_c)), const((n_state, w_c)),
            const((1, n_state)), const((1, n_state)),
            const((1, w_c)), const((w_c, w_c)), const((1, w_c)),
        ],
        out_specs=[
            pl.BlockSpec((tm, w_c), lambda b, i: (b * nt + i, 0)),
            pl.BlockSpec((None, 1, n_state), lambda b, i: (b, 0, 0)),
            pl.BlockSpec((None, 1, n_state), lambda b, i: (b, 0, 0)),
        ],
        out_shape=[
            jax.ShapeDtypeStruct((batch * seq_len, w_c), BF16),
            jax.ShapeDtypeStruct((batch, 1, n_state), F32),
            jax.ShapeDtypeStruct((batch, 1, n_state), F32),
        ],
        scratch_shapes=[
            pltpu.VMEM((tm, n_state), F32), pltpu.VMEM((tm, n_state), F32),
            pltpu.VMEM((tm, n_state), F32), pltpu.VMEM((tm, n_state), F32),
            pltpu.VMEM((SUBLANES, n_state), F32), pltpu.VMEM((SUBLANES, n_state), F32),
        ],
        compiler_params=_params(2),
        name="s5_prompt",
    )(z, z, bre.astype(BF16), bim.astype(BF16), cre.astype(BF16), cim.astype(BF16), are, aim,
      d_vec, w_glu.astype(BF16), b_glu)


def _s5_sample_body(u_ref, gate_ref, h0r_ref, h0i_ref, bre_ref, bim_ref, cre_ref, cim_ref, are_ref, aim_ref,
                    d_ref, wg_ref, bg_ref, o_ref, hre_ref, him_ref, *, n_steps, w_c):
    hr = h0r_ref[...]
    hi = h0i_ref[...]
    ar = are_ref[...]
    ai = aim_ref[...]
    for t in range(n_steps):
        sl = slice(t * w_c, (t + 1) * w_c)
        u = u_ref[:, sl]
        bur = jnp.dot(u, bre_ref[...], precision=HIGHEST, preferred_element_type=F32)
        bui = jnp.dot(u, bim_ref[...], precision=HIGHEST, preferred_element_type=F32)
        hr, hi = ar * hr - ai * hi + bur, ar * hi + ai * hr + bui
        y = (jnp.dot(hr.astype(BF16), cre_ref[...], preferred_element_type=F32)
             - jnp.dot(hi.astype(BF16), cim_ref[...], preferred_element_type=F32)
             + d_ref[...] * u)
        zz = _gelu_tanh(y)
        glu = zz * _sigmoid(jnp.dot(zz.astype(BF16), wg_ref[...], preferred_element_type=F32) + bg_ref[...])
        o_ref[:, sl] = (glu * _silu(gate_ref[:, sl])).astype(BF16)
    hre_ref[...] = hr
    him_ref[...] = hi


def _s5_sample(u, gate, h0r, h0i, mats, d_vec, w_glu, b_glu, *, n_steps, w_c):
    bre, bim, cre, cim, are, aim = mats
    n_seq = u.shape[0]
    n_state = bre.shape[1]
    body = functools.partial(_s5_sample_body, n_steps=n_steps, w_c=w_c)
    return pl.pallas_call(
        body,
        out_shape=[
            jax.ShapeDtypeStruct((n_seq, n_steps * w_c), BF16),
            jax.ShapeDtypeStruct((n_seq, n_state), F32),
            jax.ShapeDtypeStruct((n_seq, n_state), F32),
        ],
        compiler_params=pltpu.CompilerParams(vmem_limit_bytes=VMEM_LIMIT),
        name="s5_sample",
    )(u, gate, h0r, h0i, bre, bim, cre.astype(BF16), cim.astype(BF16), are, aim, d_vec, w_glu.astype(BF16), b_glu)


def _merge_body(x_ref, mf_ref, md_ref, ms_ref, w_ref, o_ref, *, w_a, w_b):
    acc = jnp.dot(mf_ref[...].astype(BF16), w_ref[0:w_a, :], preferred_element_type=F32)
    acc += jnp.dot(md_ref[...].astype(BF16), w_ref[w_a:w_a + w_b, :], preferred_element_type=F32)
    acc += jnp.dot(ms_ref[...].astype(BF16), w_ref[w_a + w_b:, :], preferred_element_type=F32)
    o_ref[...] = x_ref[...] + acc


def _merge(x, mf, md, ms, w_out_bf16):
    rows, d = x.shape
    w_a, w_b, w_c = mf.shape[1], md.shape[1], ms.shape[1]
    tm = min(512, rows)
    body = functools.partial(_merge_body, w_a=w_a, w_b=w_b)
    return pl.pallas_call(
        body,
        grid=(rows // tm,),
        in_specs=[
            pl.BlockSpec((tm, d), lambda m: (m, 0)),
            pl.BlockSpec((tm, w_a), lambda m: (m, 0)),
            pl.BlockSpec((tm, w_b), lambda m: (m, 0)),
            pl.BlockSpec((tm, w_c), lambda m: (m, 0)),
            pl.BlockSpec((w_a + w_b + w_c, d), lambda m: (0, 0)),
        ],
        out_specs=pl.BlockSpec((tm, d), lambda m: (m, 0)),
        out_shape=jax.ShapeDtypeStruct((rows, d), F32),
        compiler_params=_params(1),
        name="merge",
    )(x, mf, md, ms, w_out_bf16)


def _logf_suffix_body(x_ref, se_ref, tot_ref, *, n_heads):
    x = x_ref[...]
    width = x.shape[1]
    lane = lax.broadcasted_iota(jnp.int32, x.shape, 1)
    si = x
    tot = x
    s = n_heads
    while s < width:
        si = si + jnp.where(lane < width - s, pltpu.roll(si, width - s, 1), 0.0)
        tot = tot + pltpu.roll(tot, s, 1)
        s *= 2
    se_ref[...] = si - x
    tot_ref[...] = tot


def _logf_suffix(logf_pages, *, n_heads):
    rows, width = logf_pages.shape
    tr = max(t for t in range(SUBLANES, 256 + 1, SUBLANES) if rows % t == 0)
    return pl.pallas_call(
        functools.partial(_logf_suffix_body, n_heads=n_heads),
        grid=(rows // tr,),
        in_specs=[pl.BlockSpec((tr, width), lambda r: (r, 0))],
        out_specs=[pl.BlockSpec((tr, width), lambda r: (r, 0))] * 2,
        out_shape=[jax.ShapeDtypeStruct((rows, width), F32)] * 2,
        compiler_params=_params(1),
        name="logf_suffix",
    )(logf_pages)


def _softmax_block(scores, values, m_scr, l_scr, acc_scr):
    m_prev = m_scr[...]
    m_cur = functools.reduce(jnp.maximum, [jnp.max(s, axis=-1, keepdims=True) for s in scores])
    m_new = jnp.maximum(m_prev, m_cur)
    alpha = jnp.exp(m_prev - m_new)
    l_new = alpha * l_scr[...]
    acc = alpha * acc_scr[...]
    for s, v in zip(scores, values):
        p = jnp.exp(s - m_new[:, 0:1])
        l_new = l_new + jnp.sum(p, axis=-1, keepdims=True)
        acc = acc + jnp.dot(p, v, preferred_element_type=F32)
    m_scr[...] = m_new
    l_scr[...] = l_new
    acc_scr[...] = acc


def _sample_attn_body(pt_ref, *refs, n_q, pps, n_fh, n_dh, hd_b, lambda_init):
    del pt_ref
    n_in = 10 + 6 * pps
    (fq_ref, fkn_ref, fvn_ref, fg_ref, dq_ref, dkn_ref, dvn_ref, dg_ref, lfn_ref, sb_ref) = refs[:10]
    page_refs = refs[10:n_in]
    lam_ref, sub_ref, of_ref, od_ref = refs[n_in:n_in + 4]
    (qd_scr, cq_scr, r_scr, fkn_scr, fvn_scr, dkn_scr, dvn_scr,
     fm_scr, fl_scr, facc_scr, dm_scr, dl_scr, dacc_scr) = refs[n_in + 4:]
    j = pl.program_id(1)
    nj = pl.num_programs(1)
    rf = n_q * n_fh
    rd = n_q * n_dh
    wf = PAGE * n_fh
    fscale = LANES ** -0.5
    dscale = hd_b ** -0.5

    @pl.when(j == 0)
    def _():
        fm_scr[...] = jnp.full_like(fm_scr, NEG)
        fl_scr[...] = jnp.zeros_like(fl_scr)
        facc_scr[...] = jnp.zeros_like(facc_scr)
        dm_scr[...] = jnp.full_like(dm_scr, NEG)
        dl_scr[...] = jnp.zeros_like(dl_scr)
        dacc_scr[...] = jnp.zeros_like(dacc_scr)
        r_scr[...] = jnp.zeros_like(r_scr)
        dq = dq_ref[...]
        dlane = lax.broadcasted_iota(jnp.int32, (rd, LANES), 1)
        qd_scr[0:rd, :] = jnp.where(dlane < hd_b, dq, 0.0)
        qd_scr[rd:2 * rd, :] = jnp.where(dlane >= hd_b, dq, 0.0)
        fkn_scr[...] = jnp.zeros_like(fkn_scr)
        fvn_scr[...] = jnp.zeros_like(fvn_scr)
        dkn_scr[...] = jnp.zeros_like(dkn_scr)
        dvn_scr[...] = jnp.zeros_like(dvn_scr)
        fkn_scr[0:rf, :] = fkn_ref[...]
        fvn_scr[0:rf, :] = fvn_ref[...]
        dkn_scr[0:rd, :] = dkn_ref[...]
        dvn_scr[0:rd, :] = dvn_ref[...]
        ln = lax.broadcasted_iota(jnp.int32, (1, LANES), 1)
        cn = lfn_ref[...]
        s = n_fh
        while s < rf:
            cn = cn + jnp.where(ln >= s, pltpu.roll(cn, s, 1), 0.0)
            s *= 2
        rr = lax.broadcasted_iota(jnp.int32, (rf, LANES), 0)
        ll = lax.broadcasted_iota(jnp.int32, (rf, LANES), 1)
        cq = jnp.sum(jnp.where(ll == rr, cn, 0.0), axis=-1, keepdims=True)
        cq_scr[...] = jnp.broadcast_to(cq, (rf, LANES))
        ok = (ll < rf) & (ll % n_fh == rr % n_fh) & (ll // n_fh <= rr // n_fh)
        sf = lax.dot_general(fq_ref[...], fkn_scr[...], _NT, preferred_element_type=F32) * fscale
        sf = jnp.where(ok, sf + cq - cn, NEG)
        _softmax_block([sf], [fvn_scr[...]], fm_scr, fl_scr, facc_scr)
        sd = lax.dot_general(qd_scr[...], dkn_scr[...], _NT, preferred_element_type=F32) * dscale
        sd = sd + sb_ref[1][:, 0:LANES]
        _softmax_block([sd], [dvn_scr[...]], dm_scr, dl_scr, dacc_scr)

    frow = lax.broadcasted_iota(jnp.int32, (rf, wf), 0)
    flane = lax.broadcasted_iota(jnp.int32, (rf, wf), 1)
    own = (flane % n_fh) == (frow % n_fh)
    cq = cq_scr[:, 0:1]
    fq = fq_ref[...]
    qd = qd_scr[...]
    rsum = r_scr[...]
    first_kind = jnp.where(j == 0, 0, 2)
    f_scores, f_values, d_scores, d_values = [], [], [], []
    for g in range(pps):
        fk_ref, fv_ref, dk_ref, dv_ref, se_ref, tot_ref = page_refs[6 * g:6 * g + 6]
        sf = lax.dot_general(fq, fk_ref[...], _NT, preferred_element_type=F32) * fscale
        f_scores.append(jnp.where(own, sf + cq + (se_ref[...] + rsum), NEG))
        f_values.append(fv_ref[...])
        rsum = rsum + tot_ref[...]
        sbias = sb_ref[first_kind] if g == 0 else sb_ref[2]
        sd = lax.dot_general(qd, dk_ref[...], _NT, preferred_element_type=F32) * dscale
        d_scores.append(sd + sbias)
        d_values.append(dv_ref[...])
    r_scr[...] = rsum
    _softmax_block(f_scores, f_values, fm_scr, fl_scr, facc_scr)
    _softmax_block(d_scores, d_values, dm_scr, dl_scr, dacc_scr)

    @pl.when(j == nj - 1)
    def _():
        of_ref[...] = (facc_scr[...] / fl_scr[...]) * _silu(fg_ref[...])
        lam = _diff_lambda(lam_ref, lambda_init)
        o = dacc_scr[...] / dl_scr[...]
        od = o[0:rd, :] - lam * o[rd:2 * rd, :]
        ms = jnp.mean(od * od, axis=-1, keepdims=True)
        od = od * lax.rsqrt(ms + EPS) * sub_ref[...] * (1.0 - lambda_init)
        od_ref[...] = od * _silu(dg_ref[...])


def _sample_attn(page_table, new_rows, lfn, sbias, caches, lam_vecs, sub_gain, *, layer, n_q, n_fh, n_dh, hd_b,
                 lambda_init):
    fk_c, fv_c, dk_c, dv_c, se_c, tot_c = caches
    fq, fkn, fvn, fg, dq, dkn, dvn, dg = new_rows
    n_seq = fq.shape[0]
    n_pages = page_table.shape[1]
    pps = max(p for p in (4, 2, 1) if n_pages % p == 0)
    nj = n_pages // pps
    rf = n_q * n_fh
    rd = n_q * n_dh
    wf = PAGE * n_fh
    wd = PAGE * n_dh
    body = functools.partial(_sample_attn_body, n_q=n_q, pps=pps, n_fh=n_fh, n_dh=n_dh, hd_b=hd_b,
                             lambda_init=lambda_init)
    seq_spec = lambda r: pl.BlockSpec((None, r, LANES), lambda s, j, pt: (s, 0, 0))

    def page_specs(g):
        idx = lambda s, j, pt: (layer, pt[s, n_pages - 1 - (j * pps + g)], 0, 0)
        return [
            pl.BlockSpec((None, None, wf, LANES), idx), pl.BlockSpec((None, None, wf, LANES), idx),
            pl.BlockSpec((None, None, wd, LANES), idx), pl.BlockSpec((None, None, wd, LANES), idx),
            pl.BlockSpec((None, None, 1, wf), idx), pl.BlockSpec((None, None, 1, wf), idx),
        ]

    in_specs = [seq_spec(rf)] * 4 + [seq_spec(rd)] * 4 + [
        seq_spec(1),
        pl.BlockSpec(sbias.shape, lambda s, j, pt: (0, 0, 0)),
    ]
    args = [fq, fkn, fvn, fg, dq, dkn, dvn, dg, lfn, sbias]
    for g in range(pps):
        in_specs += page_specs(g)
        args += [fk_c, fv_c, dk_c, dv_c, se_c, tot_c]
    in_specs += [pl.BlockSpec((4, hd_b), lambda s, j, pt: (0, 0)), pl.BlockSpec((1, 2 * hd_b), lambda s, j, pt: (0, 0))]
    args += [lam_vecs, sub_gain]
    grid_spec = pltpu.PrefetchScalarGridSpec(
        num_scalar_prefetch=1,
        grid=(n_seq, nj),
        in_specs=in_specs,
        out_specs=[seq_spec(rf), seq_spec(rd)],
        scratch_shapes=[
            pltpu.VMEM((2 * rd, LANES), F32),
            pltpu.VMEM((rf, LANES), F32),
            pltpu.VMEM((1, wf), F32),
            pltpu.VMEM((PAGE, LANES), F32), pltpu.VMEM((PAGE, LANES), F32),
            pltpu.VMEM((PAGE, LANES), F32), pltpu.VMEM((PAGE, LANES), F32),
            pltpu.VMEM((rf, LANES), F32), pltpu.VMEM((rf, LANES), F32), pltpu.VMEM((rf, LANES), F32),
            pltpu.VMEM((2 * rd, LANES), F32), pltpu.VMEM((2 * rd, LANES), F32), pltpu.VMEM((2 * rd, LANES), F32),
        ],
    )
    return pl.pallas_call(
        body,
        grid_spec=grid_spec,
        out_shape=[
            jax.ShapeDtypeStruct((n_seq, rf, LANES), F32),
            jax.ShapeDtypeStruct((n_seq, rd, LANES), F32),
        ],
        compiler_params=_params(2),
        name="sample_attn",
    )(page_table, *args)


def kernel(x_prompt, x_sample, cache_fox_k, cache_fox_v, cache_fox_logf, cache_diff_k, cache_diff_v, state_s5_re, state_s5_im, page_table, norm_gain, w_in, fox_b_f, fox_q_gain, fox_k_gain, diff_q_gain, diff_k_gain, diff_lambda_q1, diff_lambda_k1, diff_lambda_q2, diff_lambda_k2, diff_subln_gain, rel_bias, s5_lambda_re, s5_lambda_im, s5_b_re, s5_b_im, s5_c_re, s5_c_im, s5_d, s5_log_step, s5_w_glu, s5_b_glu, w_out):
    batch, seq_len, d_model = x_prompt.shape
    n_seq, n_q, _ = x_sample.shape
    depth, n_pool, page, n_fh, hd_a = cache_fox_k.shape
    n_dh = cache_diff_k.shape[3]
    hd_b = cache_diff_k.shape[4] // 2
    g_c, p_state, c_group = s5_b_re.shape[1:]
    assert page == PAGE and n_fh == SUBLANES and hd_a == LANES and 2 * hd_b == LANES
    w_a = n_fh * hd_a
    w_b = n_dh * 2 * hd_b
    w_c = g_c * c_group
    n_state = g_c * p_state
    col_d = 4 * w_a
    col_s = col_d + 4 * w_b

    xp = x_prompt.reshape(batch * seq_len, d_model)
    xs = x_sample.reshape(n_seq * n_q, d_model)

    fk_c = cache_fox_k.reshape(depth, n_pool, PAGE * n_fh, hd_a)
    fv_c = cache_fox_v.reshape(depth, n_pool, PAGE * n_fh, hd_a)
    dk_c = cache_diff_k.reshape(depth, n_pool, PAGE * n_dh, 2 * hd_b)
    dv_c = cache_diff_v.reshape(depth, n_pool, PAGE * n_dh, 2 * hd_b)
    se_c, tot_c = _logf_suffix(cache_fox_logf.reshape(depth * n_pool, PAGE * n_fh), n_heads=n_fh)
    se_c = se_c.reshape(depth, n_pool, 1, PAGE * n_fh)
    tot_c = tot_c.reshape(depth, n_pool, 1, PAGE * n_fh)

    tile = min(512, seq_len)
    bias_tiles = _bias_tiles(rel_bias, n_heads=n_dh, tq=tile, tk=tile)
    sbias = _sample_bias(rel_bias, n_heads=n_dh, n_q=n_q)
    tri = jnp.tril(jnp.ones((CUMSUM_CHUNK, CUMSUM_CHUNK), F32))

    st_p, st_s = [], []
    for l in range(depth):
        lambda_init = 0.8 - 0.6 * math.exp(-0.3 * l)
        w = w_in[l]
        w_main = jnp.concatenate([w[:, :3 * w_a], w[:, 3 * w_a + n_fh:]], axis=1).astype(BF16)
        w_f = jnp.pad(w[:, 3 * w_a:3 * w_a + n_fh], ((0, 0), (0, LANES - n_fh))).astype(BF16)
        b_f = jnp.pad(fox_b_f[l], (0, LANES - n_fh)).reshape(1, LANES)
        gains = jnp.stack([fox_q_gain[l], fox_k_gain[l], jnp.tile(diff_q_gain[l], 2), jnp.tile(diff_k_gain[l], 2)])
        g_row = norm_gain[l].reshape(1, d_model)
        w_out_b = w_out[l].astype(BF16)
        lam_vecs = jnp.stack([diff_lambda_q1[l], diff_lambda_k1[l], diff_lambda_q2[l], diff_lambda_k2[l]])
        sub_gain = diff_subln_gain[l].reshape(1, 2 * hd_b)
        are, aim, bbre, bbim = _s5_prep(s5_lambda_re[l], s5_lambda_im[l], s5_log_step[l], s5_b_re[l], s5_b_im[l])
        mats = (
            _block_diag(jnp.swapaxes(bbre, 1, 2)), _block_diag(jnp.swapaxes(bbim, 1, 2)),
            _block_diag(jnp.swapaxes(s5_c_re[l], 1, 2)), _block_diag(jnp.swapaxes(s5_c_im[l], 1, 2)),
            are.reshape(1, n_state), aim.reshape(1, n_state),
        )
        d_vec = s5_d[l].reshape(1, w_c)
        b_glu = s5_b_glu[l].reshape(1, w_c)

        z, logf, c, ct = _proj(xp, g_row, w_main, w_f, b_f, gains, tri, seq_len=seq_len, w_a=w_a, w_b=w_b)
        mf = _fox_prompt(z, c, ct, batch=batch, seq_len=seq_len, n_heads=n_fh, hd=hd_a)
        md = _diff_prompt(z, bias_tiles, lam_vecs, sub_gain, batch=batch, seq_len=seq_len, n_heads=n_dh, hd=hd_b,
                          col0=col_d, lambda_init=lambda_init)
        ms, hre, him = _s5_prompt(z, mats, d_vec, s5_w_glu[l], b_glu, batch=batch, seq_len=seq_len, col0=col_s,
                                  w_c=w_c)
        xp_new = _merge(xp, mf, md, ms, w_out_b)
        st_p.append((
            z[:, w_a:2 * w_a].reshape(batch, seq_len, n_fh, hd_a),
            z[:, 2 * w_a:3 * w_a].reshape(batch, seq_len, n_fh, hd_a),
            logf[:, :n_fh].reshape(batch, seq_len, n_fh),
            z[:, col_d + w_b:col_d + 2 * w_b].reshape(batch, seq_len, n_dh, 2 * hd_b),
            z[:, col_d + 2 * w_b:col_d + 3 * w_b].reshape(batch, seq_len, n_dh, 2 * hd_b),
            hre.reshape(batch, g_c, p_state), him.reshape(batch, g_c, p_state),
        ))

        zs, logf_s, _, _ = _proj(xs, g_row, w_main, w_f, b_f, gains, tri, seq_len=n_seq * n_q, w_a=w_a, w_b=w_b)
        head_rows = lambda col, width: zs[:, col:col + width].reshape(n_seq, n_q * width // LANES, LANES)
        new_rows = ([head_rows(i * w_a, w_a) for i in range(4)]
                    + [head_rows(col_d + i * w_b, w_b) for i in range(4)])
        lfn = jnp.pad(logf_s[:, :n_fh].reshape(n_seq, 1, n_q * n_fh), ((0, 0), (0, 0), (0, LANES - n_q * n_fh)))
        mf_s, md_s = _sample_attn(page_table, new_rows, lfn, sbias, (fk_c, fv_c, dk_c, dv_c, se_c, tot_c),
                                  lam_vecs, sub_gain, layer=l, n_q=n_q, n_fh=n_fh, n_dh=n_dh, hd_b=hd_b,
                                  lambda_init=lambda_init)
        u_s = zs[:, col_s:col_s + w_c].reshape(n_seq, n_q * w_c)
        g_s = zs[:, col_s + w_c:col_s + 2 * w_c].reshape(n_seq, n_q * w_c)
        ms_s, hre_s, him_s = _s5_sample(u_s, g_s, state_s5_re[l].reshape(n_seq, n_state),
                                        state_s5_im[l].reshape(n_seq, n_state), mats, d_vec, s5_w_glu[l], b_glu,
                                        n_steps=n_q, w_c=w_c)
        xs_new = _merge(xs, mf_s.reshape(n_seq * n_q, w_a), md_s.reshape(n_seq * n_q, w_b),
                        ms_s.reshape(n_seq * n_q, w_c), w_out_b)
        st_s.append((
            zs[:, w_a:2 * w_a].reshape(n_seq, n_q, n_fh, hd_a),
            zs[:, 2 * w_a:3 * w_a].reshape(n_seq, n_q, n_fh, hd_a),
            logf_s[:, :n_fh].reshape(n_seq, n_q, n_fh),
            zs[:, col_d + w_b:col_d + 2 * w_b].reshape(n_seq, n_q, n_dh, 2 * hd_b),
            zs[:, col_d + 2 * w_b:col_d + 3 * w_b].reshape(n_seq, n_q, n_dh, 2 * hd_b),
            hre_s.reshape(n_seq, g_c, p_state), him_s.reshape(n_seq, g_c, p_state),
        ))
        xp, xs = xp_new, xs_new

    outs_p = [jnp.stack([s[i] for s in st_p]) for i in range(7)]
    outs_s = [jnp.stack([s[i] for s in st_s]) for i in range(7)]
    return (xp.reshape(batch, seq_len, d_model), xs.reshape(n_seq, n_q, d_model), *outs_p, *outs_s)
```

```python
import functools
import math

import jax
import jax.numpy as jnp
from jax import lax
from jax.experimental import pallas as pl
from jax.experimental.pallas import tpu as pltpu

F32 = jnp.float32
BF16 = jnp.bfloat16
HIGHEST = lax.Precision.HIGHEST

EPS = 1e-6
NEG = -1e30
LANES = 128
SUBLANES = 8
PAGE = 128
N_BUCKETS = 32
REL_MAX_DIST = 128
LOG2E = math.log2(math.e)
VMEM_LIMIT = 56 * 1024 * 1024

_NT = (((1,), (1,)), ((), ()))


def _params(n_axes, vmem=VMEM_LIMIT):
    return pltpu.CompilerParams(dimension_semantics=("arbitrary",) * n_axes, vmem_limit_bytes=vmem)


def _silu(x):
    return x / (1.0 + jnp.exp(-x))


def _sigmoid(x):
    return 1.0 / (1.0 + jnp.exp(-x))


def _log_sigmoid(x):
    return jnp.minimum(x, 0.0) - jnp.log1p(jnp.exp(-jnp.abs(x)))


def _gelu_tanh(x):
    return 0.5 * x * (1.0 + jnp.tanh(math.sqrt(2.0 / math.pi) * (x + 0.044715 * (x * x * x))))


CUMSUM_CHUNK = 256


def _proj_body(x_ref, g_ref, w_ref, wf_ref, bf_ref, gains_ref, tri_ref,
               z_ref, logf_ref, c_ref, ct_ref, h_scr, carry_scr, *, tm, tn, seq_tiles, seg):
    m = pl.program_id(0)
    n = pl.program_id(1)

    @pl.when(n == 0)
    def _():
        x = x_ref[...]
        ms = jnp.mean(x * x, axis=-1, keepdims=True)
        hb = (x * lax.rsqrt(ms + EPS) * g_ref[...]).astype(BF16)
        h_scr[...] = hb
        logits = jnp.dot(hb, wf_ref[...], preferred_element_type=F32) + bf_ref[...]
        lf = _log_sigmoid(logits)
        logf_ref[...] = lf

        @pl.when(m % seq_tiles == 0)
        def _():
            carry_scr[...] = jnp.zeros_like(carry_scr)

        carry = carry_scr[0:1, :]
        for j in range(tm // CUMSUM_CHUNK):
            blk = lf[j * CUMSUM_CHUNK:(j + 1) * CUMSUM_CHUNK]
            cs = jnp.dot(tri_ref[...], blk, precision=HIGHEST, preferred_element_type=F32) + carry
            c_ref[j * CUMSUM_CHUNK:(j + 1) * CUMSUM_CHUNK, :] = cs * LOG2E
            carry = cs[CUMSUM_CHUNK - 1:CUMSUM_CHUNK, :]
        carry_scr[...] = jnp.broadcast_to(carry, carry_scr.shape)
        ct_ref[...] = c_ref[...].T[0:SUBLANES, :]

    acc = jnp.dot(h_scr[...], w_ref[...], preferred_element_type=F32)
    lane = lax.broadcasted_iota(jnp.int32, (tm, LANES), 1)

    def norm_full(row):
        gain = gains_ref[row:row + 1, :]
        for j in range(tn // LANES):
            y = acc[:, j * LANES:(j + 1) * LANES]
            ms = jnp.mean(y * y, axis=-1, keepdims=True)
            z_ref[:, j * LANES:(j + 1) * LANES] = y * lax.rsqrt(ms + EPS) * gain

    def norm_half(row):
        gain = gains_ref[row:row + 1, :]
        half = LANES // 2
        for j in range(tn // LANES):
            y = acc[:, j * LANES:(j + 1) * LANES]
            y2 = y * y
            lo = jnp.sum(jnp.where(lane < half, y2, 0.0), axis=-1, keepdims=True)
            hi = jnp.sum(jnp.where(lane >= half, y2, 0.0), axis=-1, keepdims=True)
            ms = jnp.where(lane < half, lo, hi) * (1.0 / half)
            z_ref[:, j * LANES:(j + 1) * LANES] = y * lax.rsqrt(ms + EPS) * gain

    fq0, fk0, fv0, dq0, dk0, dv0 = seg

    @pl.when(n < fk0)
    def _():
        norm_full(0)

    @pl.when((n >= fk0) & (n < fv0))
    def _():
        norm_full(1)

    @pl.when(n == dq0)
    def _():
        norm_half(2)

    @pl.when(n == dk0)
    def _():
        norm_half(3)

    @pl.when(((n >= fv0) & (n < dq0)) | (n >= dv0))
    def _():
        z_ref[...] = acc


def _proj(x, g, w_main, w_f, b_f, gains, tri, *, seq_len, w_a, w_b):
    rows, d = x.shape
    n_cols = w_main.shape[1]
    tm = min(1024, rows)
    tn = 512
    assert rows % tm == 0 and n_cols % tn == 0 and seq_len % tm == 0 and tm % CUMSUM_CHUNK == 0
    assert w_a % tn == 0 and w_b == tn
    seg = (0, w_a // tn, 2 * w_a // tn, 4 * w_a // tn, 4 * w_a // tn + 1, 4 * w_a // tn + 2)
    body = functools.partial(_proj_body, tm=tm, tn=tn, seq_tiles=seq_len // tm, seg=seg)
    return pl.pallas_call(
        body,
        grid=(rows // tm, n_cols // tn),
        in_specs=[
            pl.BlockSpec((tm, d), lambda m, n: (m, 0)),
            pl.BlockSpec((1, d), lambda m, n: (0, 0)),
            pl.BlockSpec((d, tn), lambda m, n: (0, n)),
            pl.BlockSpec((d, LANES), lambda m, n: (0, 0)),
            pl.BlockSpec((1, LANES), lambda m, n: (0, 0)),
            pl.BlockSpec((4, LANES), lambda m, n: (0, 0)),
            pl.BlockSpec((CUMSUM_CHUNK, CUMSUM_CHUNK), lambda m, n: (0, 0)),
        ],
        out_specs=[
            pl.BlockSpec((tm, tn), lambda m, n: (m, n)),
            pl.BlockSpec((tm, LANES), lambda m, n: (m, 0)),
            pl.BlockSpec((tm, LANES), lambda m, n: (m, 0)),
            pl.BlockSpec((SUBLANES, tm), lambda m, n: (0, m)),
        ],
        out_shape=[
            jax.ShapeDtypeStruct((rows, n_cols), F32),
            jax.ShapeDtypeStruct((rows, LANES), F32),
            jax.ShapeDtypeStruct((rows, LANES), F32),
            jax.ShapeDtypeStruct((SUBLANES, rows), F32),
        ],
        scratch_shapes=[pltpu.VMEM((tm, d), BF16), pltpu.VMEM((SUBLANES, LANES), F32)],
        compiler_params=_params(2),
        name="proj",
    )(x, g, w_main, w_f, b_f, gains, tri)


def _fox_body(q_ref, k_ref, v_ref, c_ref, ct_ref, gate_ref, o_ref, m_scr, l_scr, acc_scr, *, n_heads, hd, tq, tk):
    qi = pl.program_id(1)
    ki = pl.program_id(2)
    nk = pl.num_programs(2)

    @pl.when(ki == 0)
    def _():
        m_scr[...] = jnp.full_like(m_scr, NEG)
        l_scr[...] = jnp.zeros_like(l_scr)
        acc_scr[...] = jnp.zeros_like(acc_scr)

    def step(diagonal):
        if diagonal:
            visible = (lax.broadcasted_iota(jnp.int32, (tq, tk), 1) <= lax.broadcasted_iota(jnp.int32, (tq, tk), 0))
        for h in range(n_heads):
            sl = slice(h * hd, (h + 1) * hd)
            q = q_ref[:, sl].astype(BF16)
            k = k_ref[:, sl].astype(BF16)
            s = lax.dot_general(q, k, _NT, preferred_element_type=F32) - ct_ref[h:h + 1, :]
            if diagonal:
                s = jnp.where(visible, s, NEG)
            cq = c_ref[:, h:h + 1]
            m_prev = m_scr[h]
            m_new = jnp.maximum(m_prev, jnp.max(s, axis=-1, keepdims=True) + cq)
            alpha = jnp.exp2(m_prev - m_new)
            p = jnp.exp2(s + (cq - m_new[:, 0:1]))
            l_scr[h] = alpha * l_scr[h] + jnp.sum(p, axis=-1, keepdims=True)
            pv = jnp.dot(p.astype(BF16), v_ref[:, sl].astype(BF16), preferred_element_type=F32)
            acc_scr[:, sl] = alpha * acc_scr[:, sl] + pv
            m_scr[h] = m_new

    @pl.when(ki < qi)
    def _():
        step(False)

    @pl.when(ki == qi)
    def _():
        step(True)

    @pl.when(ki == nk - 1)
    def _():
        for h in range(n_heads):
            sl = slice(h * hd, (h + 1) * hd)
            o = acc_scr[:, sl] / l_scr[h]
            o_ref[:, sl] = (o * _silu(gate_ref[:, sl])).astype(BF16)


def _fox_prompt(z, c, ct, *, batch, seq_len, n_heads, hd):
    w = n_heads * hd
    tq = tk = min(512, seq_len)
    nq = seq_len // tq
    body = functools.partial(_fox_body, n_heads=n_heads, hd=hd, tq=tq, tk=tk)
    kv_row = lambda b, qi, ki: b * nq + jnp.minimum(ki, qi)
    return pl.pallas_call(
        body,
        grid=(batch, nq, nq),
        in_specs=[
            pl.BlockSpec((tq, w), lambda b, qi, ki: (b * nq + qi, 0)),
            pl.BlockSpec((tk, w), lambda b, qi, ki: (kv_row(b, qi, ki), 1)),
            pl.BlockSpec((tk, w), lambda b, qi, ki: (kv_row(b, qi, ki), 2)),
            pl.BlockSpec((tq, LANES), lambda b, qi, ki: (b * nq + qi, 0)),
            pl.BlockSpec((SUBLANES, tk), lambda b, qi, ki: (0, kv_row(b, qi, ki))),
            pl.BlockSpec((tq, w), lambda b, qi, ki: (b * nq + qi, 3)),
        ],
        out_specs=pl.BlockSpec((tq, w), lambda b, qi, ki: (b * nq + qi, 0)),
        out_shape=jax.ShapeDtypeStruct((batch * seq_len, w), BF16),
        scratch_shapes=[
            pltpu.VMEM((n_heads, tq, LANES), F32),
            pltpu.VMEM((n_heads, tq, LANES), F32),
            pltpu.VMEM((tq, w), F32),
        ],
        compiler_params=_params(3),
        name="fox_prompt",
    )(z, z, z, c, ct, z)


def _rel_bucket(d):
    max_exact = N_BUCKETS // 2
    df = jnp.maximum(d, 1).astype(F32)
    large = max_exact + (jnp.log(df / max_exact) / math.log(REL_MAX_DIST / max_exact)
                         * (N_BUCKETS - max_exact)).astype(jnp.int32)
    large = jnp.minimum(large, N_BUCKETS - 1)
    return jnp.where(d < max_exact, d, large)


def _bias_lookup(rel_ref, d, h):
    bucket = _rel_bucket(jnp.maximum(d, 0))
    val = jnp.full(d.shape, rel_ref[0, h], F32)
    for b in range(1, N_BUCKETS):
        val = jnp.where(bucket == b, rel_ref[b, h], val)
    return jnp.where(d >= 0, val * LOG2E, NEG)


def _bias_tiles_body(rel_ref, o_ref, *, tq, tk):
    h = pl.program_id(0)
    kind = pl.program_id(1)
    row = lax.broadcasted_iota(jnp.int32, (tq, tk), 0)
    col = lax.broadcasted_iota(jnp.int32, (tq, tk), 1)
    o_ref[...] = _bias_lookup(rel_ref, kind * tq + row - col, h)


def _bias_tiles(rel_bias, *, n_heads, tq, tk):
    return pl.pallas_call(
        functools.partial(_bias_tiles_body, tq=tq, tk=tk),
        grid=(n_heads, 3),
        in_specs=[pl.BlockSpec(memory_space=pltpu.SMEM)],
        out_specs=pl.BlockSpec((None, None, tq, tk), lambda h, kind: (h, kind, 0, 0)),
        out_shape=jax.ShapeDtypeStruct((n_heads, 3, tq, tk), F32),
        compiler_params=_params(2),
        name="bias_tiles",
    )(rel_bias)


def _sample_bias_body(rel_ref, o_ref, *, n_q, n_heads):
    kind = pl.program_id(0)
    rows = 2 * n_q * n_heads
    width = PAGE * n_heads
    r = lax.broadcasted_iota(jnp.int32, (rows, width), 0)
    lane = lax.broadcasted_iota(jnp.int32, (rows, width), 1)
    i = (r % (n_q * n_heads)) // n_heads
    pos = lane // n_heads
    d_last = PAGE + i - pos
    d_new = jnp.where(pos < n_q, i - pos, -1)
    d_far = jnp.full((rows, width), 2 * PAGE, jnp.int32)
    d = jnp.where(kind == 0, d_last, jnp.where(kind == 1, d_new, d_far))
    out = jnp.full((rows, width), NEG, F32)
    for h in range(n_heads):
        own = (lane % n_heads == h) & (r % n_heads == h)
        out = jnp.where(own, _bias_lookup(rel_ref, d, h), out)
    o_ref[...] = out


def _sample_bias(rel_bias, *, n_heads, n_q):
    rows = 2 * n_q * n_heads
    width = PAGE * n_heads
    return pl.pallas_call(
        functools.partial(_sample_bias_body, n_q=n_q, n_heads=n_heads),
        grid=(3,),
        in_specs=[pl.BlockSpec(memory_space=pltpu.SMEM)],
        out_specs=pl.BlockSpec((None, rows, width), lambda kind: (kind, 0, 0)),
        out_shape=jax.ShapeDtypeStruct((3, rows, width), F32),
        compiler_params=_params(1),
        name="sample_bias",
    )(rel_bias)


def _diff_lambda(lam_ref, lambda_init):
    lv = lam_ref[...]
    e1 = jnp.exp(jnp.sum(lv[0:1, :] * lv[1:2, :], axis=-1, keepdims=True))
    e2 = jnp.exp(jnp.sum(lv[2:3, :] * lv[3:4, :], axis=-1, keepdims=True))
    return e1 - e2 + lambda_init


def _diff_body(q_ref, k_ref, v_ref, bias_ref, gate_ref, lam_ref, sub_ref, o_ref,
               m_scr, l_scr, acc_scr, *, n_heads, hd, tq, tk, lambda_init):
    qi = pl.program_id(1)
    ki = pl.program_id(2)
    nk = pl.num_programs(2)
    hw = 2 * hd

    @pl.when(ki == 0)
    def _():
        m_scr[...] = jnp.full_like(m_scr, NEG)
        l_scr[...] = jnp.zeros_like(l_scr)
        acc_scr[...] = jnp.zeros_like(acc_scr)

    @pl.when(ki <= qi)
    def _():
        lane = lax.broadcasted_iota(jnp.int32, (tq, hw), 1)
        for h in range(n_heads):
            sl = slice(h * hw, (h + 1) * hw)
            q = q_ref[:, sl]
            k = k_ref[:, sl].astype(BF16)
            v = v_ref[:, sl].astype(BF16)
            bias = bias_ref[h]
            for mp in range(2):
                qm = jnp.where((lane < hd) if mp == 0 else (lane >= hd), q, 0.0).astype(BF16)
                s = lax.dot_general(qm, k, _NT, preferred_element_type=F32) + bias
                idx = 2 * h + mp
                m_prev = m_scr[idx]
                m_new = jnp.maximum(m_prev, jnp.max(s, axis=-1, keepdims=True))
                alpha = jnp.exp2(m_prev - m_new)
                p = jnp.exp2(s - m_new[:, 0:1])
                l_scr[idx] = alpha * l_scr[idx] + jnp.sum(p, axis=-1, keepdims=True)
                pv = jnp.dot(p.astype(BF16), v, preferred_element_type=F32)
                acc_scr[idx] = alpha * acc_scr[idx] + pv
                m_scr[idx] = m_new

    @pl.when(ki == nk - 1)
    def _():
        lam = _diff_lambda(lam_ref, lambda_init)
        for h in range(n_heads):
            sl = slice(h * hw, (h + 1) * hw)
            o = acc_scr[2 * h] / l_scr[2 * h] - lam * (acc_scr[2 * h + 1] / l_scr[2 * h + 1])
            ms = jnp.mean(o * o, axis=-1, keepdims=True)
            o = o * lax.rsqrt(ms + EPS) * sub_ref[...] * (1.0 - lambda_init)
            o_ref[:, sl] = (o * _silu(gate_ref[:, sl])).astype(BF16)


def _diff_prompt(z, bias_tiles, lam_vecs, sub_gain, *, batch, seq_len, n_heads, hd, col0, lambda_init):
    hw = 2 * hd
    w = n_heads * hw
    tq = tk = min(512, seq_len)
    nq = seq_len // tq
    cb = col0 // w
    body = functools.partial(_diff_body, n_heads=n_heads, hd=hd, tq=tq, tk=tk, lambda_init=lambda_init)
    kv_row = lambda b, qi, ki: b * nq + jnp.minimum(ki, qi)
    kind = lambda b, qi, ki: jnp.clip(qi - jnp.minimum(ki, qi), 0, 2)
    return pl.pallas_call(
        body,
        grid=(batch, nq, nq),
        in_specs=[
            pl.BlockSpec((tq, w), lambda b, qi, ki: (b * nq + qi, cb)),
            pl.BlockSpec((tk, w), lambda b, qi, ki: (kv_row(b, qi, ki), cb + 1)),
            pl.BlockSpec((tk, w), lambda b, qi, ki: (kv_row(b, qi, ki), cb + 2)),
            pl.BlockSpec((n_heads, None, tq, tk), lambda b, qi, ki: (0, kind(b, qi, ki), 0, 0)),
            pl.BlockSpec((tq, w), lambda b, qi, ki: (b * nq + qi, cb + 3)),
            pl.BlockSpec((4, hd), lambda b, qi, ki: (0, 0)),
            pl.BlockSpec((1, hw), lambda b, qi, ki: (0, 0)),
        ],
        out_specs=pl.BlockSpec((tq, w), lambda b, qi, ki: (b * nq + qi, 0)),
        out_shape=jax.ShapeDtypeStruct((batch * seq_len, w), BF16),
        scratch_shapes=[
            pltpu.VMEM((2 * n_heads, tq, LANES), F32),
            pltpu.VMEM((2 * n_heads, tq, LANES), F32),
            pltpu.VMEM((2 * n_heads, tq, hw), F32),
        ],
        compiler_params=_params(3),
        name="diff_prompt",
    )(z, z, z, bias_tiles, z, lam_vecs, sub_gain)


def _s5_prep_body(lre_ref, lim_ref, ls_ref, bre_ref, bim_ref, are_ref, aim_ref, bbre_ref, bbim_ref):
    lr = lre_ref[...]
    li = lim_ref[...]
    step = jnp.exp(ls_ref[...])
    mag = jnp.exp(lr * step)
    ar = mag * jnp.cos(li * step)
    ai = mag * jnp.sin(li * step)
    den = lr * lr + li * li
    nr = ar - 1.0
    cr = (nr * lr + ai * li) / den
    ci = (ai * lr - nr * li) / den
    br = bre_ref[...]
    bi = bim_ref[...]
    are_ref[...] = ar
    aim_ref[...] = ai
    bbre_ref[...] = cr * br - ci * bi
    bbim_ref[...] = cr * bi + ci * br


def _s5_prep(lam_re, lam_im, log_step, b_re, b_im):
    g, p, c = b_re.shape
    col = lambda a: a.reshape(g, p, 1)
    ls = jnp.broadcast_to(log_step.reshape(g, 1, 1), (g, p, 1))
    full3 = lambda shp: pl.BlockSpec(shp, lambda: (0, 0, 0))
    return pl.pallas_call(
        _s5_prep_body,
        in_specs=[full3((g, p, 1))] * 3 + [full3((g, p, c))] * 2,
        out_specs=[full3((g, p, 1))] * 2 + [full3((g, p, c))] * 2,
        out_shape=[jax.ShapeDtypeStruct((g, p, 1), F32)] * 2 + [jax.ShapeDtypeStruct((g, p, c), F32)] * 2,
        name="s5_prep",
    )(col(lam_re), col(lam_im), ls, b_re, b_im)


def _block_diag(blocks):
    g, r, c = blocks.shape
    eye = jnp.eye(g, dtype=blocks.dtype)
    return (blocks[:, :, None, :] * eye[:, None, :, None]).reshape(g * r, g * c)


S5_LANE_CHUNK = 256


def _s5_scan_body(u_ref, gate_ref, bre_ref, bim_ref, cre_ref, cim_ref, are_ref, aim_ref, d_ref, wg_ref, bg_ref,
                  o_ref, hre_ref, him_ref, pwr_scr, pwi_scr, hr_scr, hi_scr, cr_scr, ci_scr, *, tm, n_state):
    b = pl.program_id(0)
    i = pl.program_id(1)
    nt = pl.num_programs(1)
    n_levels = tm.bit_length() - 1

    @pl.when((b == 0) & (i == 0))
    def _():
        pwr_scr[0:1, :] = are_ref[...]
        pwi_scr[0:1, :] = aim_ref[...]
        for lv in range(n_levels):
            s = 1 << lv
            tr = pwr_scr[s - 1:s, :]
            ti = pwi_scr[s - 1:s, :]
            xr = pwr_scr[0:s, :]
            xi = pwi_scr[0:s, :]
            pwr_scr[s:2 * s, :] = xr * tr - xi * ti
            pwi_scr[s:2 * s, :] = xr * ti + xi * tr

    @pl.when(i == 0)
    def _():
        cr_scr[...] = jnp.zeros_like(cr_scr)
        ci_scr[...] = jnp.zeros_like(ci_scr)

    u = u_ref[...]
    ub = u.astype(BF16)
    hr_scr[...] = jnp.dot(ub, bre_ref[...], preferred_element_type=F32)
    hi_scr[...] = jnp.dot(ub, bim_ref[...], preferred_element_type=F32)

    row = lax.broadcasted_iota(jnp.int32, (tm, S5_LANE_CHUNK), 0)

    def chunk(j, carry):
        ln = pl.ds(pl.multiple_of(j * S5_LANE_CHUNK, S5_LANE_CHUNK), S5_LANE_CHUNK)
        hr = hr_scr[:, ln]
        hi = hi_scr[:, ln]
        for lv in range(n_levels):
            s = 1 << lv
            mr = pwr_scr[s - 1:s, ln]
            mi = pwi_scr[s - 1:s, ln]
            sr = jnp.where(row >= s, pltpu.roll(hr, s, 0), 0.0)
            si = jnp.where(row >= s, pltpu.roll(hi, s, 0), 0.0)
            hr, hi = hr + (mr * sr - mi * si), hi + (mr * si + mi * sr)
        pr = cr_scr[0:1, ln]
        pi = ci_scr[0:1, ln]
        wr = pwr_scr[:, ln]
        wi = pwi_scr[:, ln]
        hr, hi = hr + (wr * pr - wi * pi), hi + (wr * pi + wi * pr)
        hr_scr[:, ln] = hr
        hi_scr[:, ln] = hi
        cr_scr[:, ln] = jnp.broadcast_to(hr[tm - 1:tm, :], (SUBLANES, S5_LANE_CHUNK))
        ci_scr[:, ln] = jnp.broadcast_to(hi[tm - 1:tm, :], (SUBLANES, S5_LANE_CHUNK))
        return carry

    lax.fori_loop(0, n_state // S5_LANE_CHUNK, chunk, 0)

    y = (jnp.dot(hr_scr[...].astype(BF16), cre_ref[...], preferred_element_type=F32)
         - jnp.dot(hi_scr[...].astype(BF16), cim_ref[...], preferred_element_type=F32)
         + d_ref[...] * u)
    zz = _gelu_tanh(y)
    glu = zz * _sigmoid(jnp.dot(zz.astype(BF16), wg_ref[...], preferred_element_type=F32) + bg_ref[...])
    o_ref[...] = (glu * _silu(gate_ref[...])).astype(BF16)

    @pl.when(i == nt - 1)
    def _():
        hre_ref[...] = cr_scr[0:1, :]
        him_ref[...] = ci_scr[0:1, :]


def _s5_prompt(z, mats, d_vec, w_glu, b_glu, *, batch, seq_len, col0, w_c):
    bre, bim, cre, cim, are, aim = mats
    n_state = bre.shape[1]
    tm = min(512, seq_len)
    nt = seq_len // tm
    cb = col0 // w_c
    body = functools.partial(_s5_scan_body, tm=tm, n_state=n_state)
    const = lambda shp: pl.BlockSpec(shp, lambda b, i: (0,) * len(shp))
    return pl.pallas_call(
        body,
        grid=(batch, nt),
        in_specs=[
            pl.BlockSpec((tm, w_c), lambda b, i: (b * nt + i, cb)),
            pl.BlockSpec((tm, w_c), lambda b, i: (b * nt + i, cb + 1)),
            const((w_c, n_state)), const((w_c, n_state)),
            const((n_state, w_c)), const((n_state, w_c)),
            const((1, n_state)), const((1, n_state)),
            const((1, w_c)), const((w_c, w_c)), const((1, w_c)),
        ],
        out_specs=[
            pl.BlockSpec((tm, w_c), lambda b, i: (b * nt + i, 0)),
            pl.BlockSpec((None, 1, n_state), lambda b, i: (b, 0, 0)),
            pl.BlockSpec((None, 1, n_state), lambda b, i: (b, 0, 0)),
        ],
        out_shape=[
            jax.ShapeDtypeStruct((batch * seq_len, w_c), BF16),
            jax.ShapeDtypeStruct((batch, 1, n_state), F32),
            jax.ShapeDtypeStruct((batch, 1, n_state), F32),
        ],
        scratch_shapes=[
            pltpu.VMEM((tm, n_state), F32), pltpu.VMEM((tm, n_state), F32),
            pltpu.VMEM((tm, n_state), F32), pltpu.VMEM((tm, n_state), F32),
            pltpu.VMEM((SUBLANES, n_state), F32), pltpu.VMEM((SUBLANES, n_state), F32),
        ],
        compiler_params=_params(2),
        name="s5_prompt",
    )(z, z, bre.astype(BF16), bim.astype(BF16), cre.astype(BF16), cim.astype(BF16), are, aim,
      d_vec, w_glu.astype(BF16), b_glu)


def _s5_sample_body(u_ref, gate_ref, h0r_ref, h0i_ref, bre_ref, bim_ref, cre_ref, cim_ref, are_ref, aim_ref,
                    d_ref, wg_ref, bg_ref, o_ref, hre_ref, him_ref, *, n_steps, w_c):
    hr = h0r_ref[...]
    hi = h0i_ref[...]
    ar = are_ref[...]
    ai = aim_ref[...]
    for t in range(n_steps):
        sl = slice(t * w_c, (t + 1) * w_c)
        u = u_ref[:, sl]
        bur = jnp.dot(u, bre_ref[...], precision=HIGHEST, preferred_element_type=F32)
        bui = jnp.dot(u, bim_ref[...], precision=HIGHEST, preferred_element_type=F32)
        hr, hi = ar * hr - ai * hi + bur, ar * hi + ai * hr + bui
        y = (jnp.dot(hr.astype(BF16), cre_ref[...], preferred_element_type=F32)
             - jnp.dot(hi.astype(BF16), cim_ref[...], preferred_element_type=F32)
             + d_ref[...] * u)
        zz = _gelu_tanh(y)
        glu = zz * _sigmoid(jnp.dot(zz.astype(BF16), wg_ref[...], preferred_element_type=F32) + bg_ref[...])
        o_ref[:, sl] = (glu * _silu(gate_ref[:, sl])).astype(BF16)
    hre_ref[...] = hr
    him_ref[...] = hi


def _s5_sample(u, gate, h0r, h0i, mats, d_vec, w_glu, b_glu, *, n_steps, w_c):
    bre, bim, cre, cim, are, aim = mats
    n_seq = u.shape[0]
    n_state = bre.shape[1]
    body = functools.partial(_s5_sample_body, n_steps=n_steps, w_c=w_c)
    return pl.pallas_call(
        body,
        out_shape=[
            jax.ShapeDtypeStruct((n_seq, n_steps * w_c), BF16),
            jax.ShapeDtypeStruct((n_seq, n_state), F32),
            jax.ShapeDtypeStruct((n_seq, n_state), F32),
        ],
        compiler_params=pltpu.CompilerParams(vmem_limit_bytes=VMEM_LIMIT),
        name="s5_sample",
    )(u, gate, h0r, h0i, bre, bim, cre.astype(BF16), cim.astype(BF16), are, aim, d_vec, w_glu.astype(BF16), b_glu)


def _merge_body(x_ref, mf_ref, md_ref, ms_ref, w_ref, o_ref, *, w_a, w_b):
    acc = jnp.dot(mf_ref[...].astype(BF16), w_ref[0:w_a, :], preferred_element_type=F32)
    acc += jnp.dot(md_ref[...].astype(BF16), w_ref[w_a:w_a + w_b, :], preferred_element_type=F32)
    acc += jnp.dot(ms_ref[...].astype(BF16), w_ref[w_a + w_b:, :], preferred_element_type=F32)
    o_ref[...] = x_ref[...] + acc


def _merge(x, mf, md, ms, w_out_bf16):
    rows, d = x.shape
    w_a, w_b, w_c = mf.shape[1], md.shape[1], ms.shape[1]
    tm = min(512, rows)
    body = functools.partial(_merge_body, w_a=w_a, w_b=w_b)
    return pl.pallas_call(
        body,
        grid=(rows // tm,),
        in_specs=[
            pl.BlockSpec((tm, d), lambda m: (m, 0)),
            pl.BlockSpec((tm, w_a), lambda m: (m, 0)),
            pl.BlockSpec((tm, w_b), lambda m: (m, 0)),
            pl.BlockSpec((tm, w_c), lambda m: (m, 0)),
            pl.BlockSpec((w_a + w_b + w_c, d), lambda m: (0, 0)),
        ],
        out_specs=pl.BlockSpec((tm, d), lambda m: (m, 0)),
        out_shape=jax.ShapeDtypeStruct((rows, d), F32),
        compiler_params=_params(1),
        name="merge",
    )(x, mf, md, ms, w_out_bf16)


def _logf_suffix_body(x_ref, se_ref, tot_ref, *, n_heads):
    x = x_ref[...]
    width = x.shape[1]
    lane = lax.broadcasted_iota(jnp.int32, x.shape, 1)
    si = x
    tot = x
    s = n_heads
    while s < width:
        si = si + jnp.where(lane < width - s, pltpu.roll(si, width - s, 1), 0.0)
        tot = tot + pltpu.roll(tot, s, 1)
        s *= 2
    se_ref[...] = (si - x) * LOG2E
    tot_ref[...] = tot * LOG2E


def _logf_suffix(logf_pages, *, n_heads):
    rows, width = logf_pages.shape
    tr = max(t for t in range(SUBLANES, 256 + 1, SUBLANES) if rows % t == 0)
    return pl.pallas_call(
        functools.partial(_logf_suffix_body, n_heads=n_heads),
        grid=(rows // tr,),
        in_specs=[pl.BlockSpec((tr, width), lambda r: (r, 0))],
        out_specs=[pl.BlockSpec((tr, width), lambda r: (r, 0))] * 2,
        out_shape=[jax.ShapeDtypeStruct((rows, width), F32)] * 2,
        compiler_params=_params(1),
        name="logf_suffix",
    )(logf_pages)


def _softmax_block(scores, values, m_scr, l_scr, acc_scr):
    m_prev = m_scr[...]
    m_cur = functools.reduce(jnp.maximum, [jnp.max(s, axis=-1, keepdims=True) for s in scores])
    m_new = jnp.maximum(m_prev, m_cur)
    alpha = jnp.exp2(m_prev - m_new)
    l_new = alpha * l_scr[...]
    acc = alpha * acc_scr[...]
    for s, v in zip(scores, values):
        p = jnp.exp2(s - m_new[:, 0:1])
        l_new = l_new + jnp.sum(p, axis=-1, keepdims=True)
        acc = acc + jnp.dot(p, v, preferred_element_type=F32)
    m_scr[...] = m_new
    l_scr[...] = l_new
    acc_scr[...] = acc


def _sample_attn_body(pt_ref, *refs, n_q, pps, n_fh, n_dh, hd_b, lambda_init):
    del pt_ref
    n_in = 10 + 6 * pps
    (fq_ref, fkn_ref, fvn_ref, fg_ref, dq_ref, dkn_ref, dvn_ref, dg_ref, lfn_ref, sb_ref) = refs[:10]
    page_refs = refs[10:n_in]
    lam_ref, sub_ref, of_ref, od_ref = refs[n_in:n_in + 4]
    (qd_scr, cq_scr, r_scr, fkn_scr, fvn_scr, dkn_scr, dvn_scr,
     fm_scr, fl_scr, facc_scr, dm_scr, dl_scr, dacc_scr) = refs[n_in + 4:]
    j = pl.program_id(1)
    nj = pl.num_programs(1)
    rf = n_q * n_fh
    rd = n_q * n_dh
    wf = PAGE * n_fh

    @pl.when(j == 0)
    def _():
        fm_scr[...] = jnp.full_like(fm_scr, NEG)
        fl_scr[...] = jnp.zeros_like(fl_scr)
        facc_scr[...] = jnp.zeros_like(facc_scr)
        dm_scr[...] = jnp.full_like(dm_scr, NEG)
        dl_scr[...] = jnp.zeros_like(dl_scr)
        dacc_scr[...] = jnp.zeros_like(dacc_scr)
        r_scr[...] = jnp.zeros_like(r_scr)
        dq = dq_ref[...]
        dlane = lax.broadcasted_iota(jnp.int32, (rd, LANES), 1)
        qd_scr[0:rd, :] = jnp.where(dlane < hd_b, dq, 0.0)
        qd_scr[rd:2 * rd, :] = jnp.where(dlane >= hd_b, dq, 0.0)
        fkn_scr[...] = jnp.zeros_like(fkn_scr)
        fvn_scr[...] = jnp.zeros_like(fvn_scr)
        dkn_scr[...] = jnp.zeros_like(dkn_scr)
        dvn_scr[...] = jnp.zeros_like(dvn_scr)
        fkn_scr[0:rf, :] = fkn_ref[...]
        fvn_scr[0:rf, :] = fvn_ref[...]
        dkn_scr[0:rd, :] = dkn_ref[...]
        dvn_scr[0:rd, :] = dvn_ref[...]
        ln = lax.broadcasted_iota(jnp.int32, (1, LANES), 1)
        cn = lfn_ref[...]
        s = n_fh
        while s < rf:
            cn = cn + jnp.where(ln >= s, pltpu.roll(cn, s, 1), 0.0)
            s *= 2
        cn = cn * LOG2E
        rr = lax.broadcasted_iota(jnp.int32, (rf, LANES), 0)
        ll = lax.broadcasted_iota(jnp.int32, (rf, LANES), 1)
        cq = jnp.sum(jnp.where(ll == rr, cn, 0.0), axis=-1, keepdims=True)
        cq_scr[...] = jnp.broadcast_to(cq, (rf, LANES))
        ok = (ll < rf) & (ll % n_fh == rr % n_fh) & (ll // n_fh <= rr // n_fh)
        sf = lax.dot_general(fq_ref[...], fkn_scr[...], _NT, preferred_element_type=F32)
        sf = jnp.where(ok, sf + cq - cn, NEG)
        _softmax_block([sf], [fvn_scr[...]], fm_scr, fl_scr, facc_scr)
        sd = lax.dot_general(qd_scr[...], dkn_scr[...], _NT, preferred_element_type=F32)
        sd = sd + sb_ref[1][:, 0:LANES]
        _softmax_block([sd], [dvn_scr[...]], dm_scr, dl_scr, dacc_scr)

    frow = lax.broadcasted_iota(jnp.int32, (rf, wf), 0)
    flane = lax.broadcasted_iota(jnp.int32, (rf, wf), 1)
    own = (flane % n_fh) == (frow % n_fh)
    cq = cq_scr[:, 0:1]
    fq = fq_ref[...]
    qd = qd_scr[...]
    rsum = r_scr[...]
    first_kind = jnp.where(j == 0, 0, 2)
    f_scores, f_values, d_scores, d_values = [], [], [], []
    for g in range(pps):
        fk_ref, fv_ref, dk_ref, dv_ref, se_ref, tot_ref = page_refs[6 * g:6 * g + 6]
        sf = lax.dot_general(fq, fk_ref[...], _NT, preferred_element_type=F32)
        f_scores.append(jnp.where(own, sf + cq + (se_ref[...] + rsum), NEG))
        f_values.append(fv_ref[...])
        rsum = rsum + tot_ref[...]
        sbias = sb_ref[first_kind] if g == 0 else sb_ref[2]
        sd = lax.dot_general(qd, dk_ref[...], _NT, preferred_element_type=F32)
        d_scores.append(sd + sbias)
        d_values.append(dv_ref[...])
    r_scr[...] = rsum
    _softmax_block(f_scores, f_values, fm_scr, fl_scr, facc_scr)
    _softmax_block(d_scores, d_values, dm_scr, dl_scr, dacc_scr)

    @pl.when(j == nj - 1)
    def _():
        of_ref[...] = (facc_scr[...] / fl_scr[...]) * _silu(fg_ref[...])
        lam = _diff_lambda(lam_ref, lambda_init)
        o = dacc_scr[...] / dl_scr[...]
        od = o[0:rd, :] - lam * o[rd:2 * rd, :]
        ms = jnp.mean(od * od, axis=-1, keepdims=True)
        od = od * lax.rsqrt(ms + EPS) * sub_ref[...] * (1.0 - lambda_init)
        od_ref[...] = od * _silu(dg_ref[...])


def _sample_attn(page_table, new_rows, lfn, sbias, caches, lam_vecs, sub_gain, *, layer, n_q, n_fh, n_dh, hd_b,
                 lambda_init):
    fk_c, fv_c, dk_c, dv_c, se_c, tot_c = caches
    fq, fkn, fvn, fg, dq, dkn, dvn, dg = new_rows
    n_seq = fq.shape[0]
    n_pages = page_table.shape[1]
    pps = max(p for p in (8, 4, 2, 1) if n_pages % p == 0)
    nj = n_pages // pps
    rf = n_q * n_fh
    rd = n_q * n_dh
    wf = PAGE * n_fh
    wd = PAGE * n_dh
    body = functools.partial(_sample_attn_body, n_q=n_q, pps=pps, n_fh=n_fh, n_dh=n_dh, hd_b=hd_b,
                             lambda_init=lambda_init)
    seq_spec = lambda r: pl.BlockSpec((None, r, LANES), lambda s, j, pt: (s, 0, 0))

    def page_specs(g):
        idx = lambda s, j, pt: (layer, pt[s, n_pages - 1 - (j * pps + g)], 0, 0)
        return [
            pl.BlockSpec((None, None, wf, LANES), idx), pl.BlockSpec((None, None, wf, LANES), idx),
            pl.BlockSpec((None, None, wd, LANES), idx), pl.BlockSpec((None, None, wd, LANES), idx),
            pl.BlockSpec((None, None, 1, wf), idx), pl.BlockSpec((None, None, 1, wf), idx),
        ]

    in_specs = [seq_spec(rf)] * 4 + [seq_spec(rd)] * 4 + [
        seq_spec(1),
        pl.BlockSpec(sbias.shape, lambda s, j, pt: (0, 0, 0)),
    ]
    args = [fq, fkn, fvn, fg, dq, dkn, dvn, dg, lfn, sbias]
    for g in range(pps):
        in_specs += page_specs(g)
        args += [fk_c, fv_c, dk_c, dv_c, se_c, tot_c]
    in_specs += [pl.BlockSpec((4, hd_b), lambda s, j, pt: (0, 0)), pl.BlockSpec((1, 2 * hd_b), lambda s, j, pt: (0, 0))]
    args += [lam_vecs, sub_gain]
    grid_spec = pltpu.PrefetchScalarGridSpec(
        num_scalar_prefetch=1,
        grid=(n_seq, nj),
        in_specs=in_specs,
        out_specs=[seq_spec(rf), seq_spec(rd)],
        scratch_shapes=[
            pltpu.VMEM((2 * rd, LANES), F32),
            pltpu.VMEM((rf, LANES), F32),
            pltpu.VMEM((1, wf), F32),
            pltpu.VMEM((PAGE, LANES), F32), pltpu.VMEM((PAGE, LANES), F32),
            pltpu.VMEM((PAGE, LANES), F32), pltpu.VMEM((PAGE, LANES), F32),
            pltpu.VMEM((rf, LANES), F32), pltpu.VMEM((rf, LANES), F32), pltpu.VMEM((rf, LANES), F32),
            pltpu.VMEM((2 * rd, LANES), F32), pltpu.VMEM((2 * rd, LANES), F32), pltpu.VMEM((2 * rd, LANES), F32),
        ],
    )
    return pl.pallas_call(
        body,
        grid_spec=grid_spec,
        out_shape=[
            jax.ShapeDtypeStruct((n_seq, rf, LANES), F32),
            jax.ShapeDtypeStruct((n_seq, rd, LANES), F32),
        ],
        compiler_params=_params(2),
        name="sample_attn",
    )(page_table, *args)


def kernel(x_prompt, x_sample, cache_fox_k, cache_fox_v, cache_fox_logf, cache_diff_k, cache_diff_v, state_s5_re, state_s5_im, page_table, norm_gain, w_in, fox_b_f, fox_q_gain, fox_k_gain, diff_q_gain, diff_k_gain, diff_lambda_q1, diff_lambda_k1, diff_lambda_q2, diff_lambda_k2, diff_subln_gain, rel_bias, s5_lambda_re, s5_lambda_im, s5_b_re, s5_b_im, s5_c_re, s5_c_im, s5_d, s5_log_step, s5_w_glu, s5_b_glu, w_out):
    batch, seq_len, d_model = x_prompt.shape
    n_seq, n_q, _ = x_sample.shape
    depth, n_pool, page, n_fh, hd_a = cache_fox_k.shape
    n_dh = cache_diff_k.shape[3]
    hd_b = cache_diff_k.shape[4] // 2
    g_c, p_state, c_group = s5_b_re.shape[1:]
    assert page == PAGE and n_fh == SUBLANES and hd_a == LANES and 2 * hd_b == LANES
    w_a = n_fh * hd_a
    w_b = n_dh * 2 * hd_b
    w_c = g_c * c_group
    n_state = g_c * p_state
    col_d = 4 * w_a
    col_s = col_d + 4 * w_b

    xp = x_prompt.reshape(batch * seq_len, d_model)
    xs = x_sample.reshape(n_seq * n_q, d_model)

    fk_c = cache_fox_k.reshape(depth, n_pool, PAGE * n_fh, hd_a)
    fv_c = cache_fox_v.reshape(depth, n_pool, PAGE * n_fh, hd_a)
    dk_c = cache_diff_k.reshape(depth, n_pool, PAGE * n_dh, 2 * hd_b)
    dv_c = cache_diff_v.reshape(depth, n_pool, PAGE * n_dh, 2 * hd_b)
    se_c, tot_c = _logf_suffix(cache_fox_logf.reshape(depth * n_pool, PAGE * n_fh), n_heads=n_fh)
    se_c = se_c.reshape(depth, n_pool, 1, PAGE * n_fh)
    tot_c = tot_c.reshape(depth, n_pool, 1, PAGE * n_fh)

    tile = min(512, seq_len)
    bias_tiles = _bias_tiles(rel_bias, n_heads=n_dh, tq=tile, tk=tile)
    sbias = _sample_bias(rel_bias, n_heads=n_dh, n_q=n_q)
    tri = jnp.tril(jnp.ones((CUMSUM_CHUNK, CUMSUM_CHUNK), F32))

    st_p, st_s = [], []
    for l in range(depth):
        lambda_init = 0.8 - 0.6 * math.exp(-0.3 * l)
        w = w_in[l]
        w_main = jnp.concatenate([w[:, :3 * w_a], w[:, 3 * w_a + n_fh:]], axis=1).astype(BF16)
        w_f = jnp.pad(w[:, 3 * w_a:3 * w_a + n_fh], ((0, 0), (0, LANES - n_fh))).astype(BF16)
        b_f = jnp.pad(fox_b_f[l], (0, LANES - n_fh)).reshape(1, LANES)
        gains = jnp.stack([fox_q_gain[l] * (hd_a ** -0.5 * LOG2E), fox_k_gain[l],
                           jnp.tile(diff_q_gain[l], 2) * (hd_b ** -0.5 * LOG2E), jnp.tile(diff_k_gain[l], 2)])
        g_row = norm_gain[l].reshape(1, d_model)
        w_out_b = w_out[l].astype(BF16)
        lam_vecs = jnp.stack([diff_lambda_q1[l], diff_lambda_k1[l], diff_lambda_q2[l], diff_lambda_k2[l]])
        sub_gain = diff_subln_gain[l].reshape(1, 2 * hd_b)
        are, aim, bbre, bbim = _s5_prep(s5_lambda_re[l], s5_lambda_im[l], s5_log_step[l], s5_b_re[l], s5_b_im[l])
        mats = (
            _block_diag(jnp.swapaxes(bbre, 1, 2)), _block_diag(jnp.swapaxes(bbim, 1, 2)),
            _block_diag(jnp.swapaxes(s5_c_re[l], 1, 2)), _block_diag(jnp.swapaxes(s5_c_im[l], 1, 2)),
            are.reshape(1, n_state), aim.reshape(1, n_state),
        )
        d_vec = s5_d[l].reshape(1, w_c)
        b_glu = s5_b_glu[l].reshape(1, w_c)

        z, logf, c, ct = _proj(xp, g_row, w_main, w_f, b_f, gains, tri, seq_len=seq_len, w_a=w_a, w_b=w_b)
        mf = _fox_prompt(z, c, ct, batch=batch, seq_len=seq_len, n_heads=n_fh, hd=hd_a)
        md = _diff_prompt(z, bias_tiles, lam_vecs, sub_gain, batch=batch, seq_len=seq_len, n_heads=n_dh, hd=hd_b,
                          col0=col_d, lambda_init=lambda_init)
        ms, hre, him = _s5_prompt(z, mats, d_vec, s5_w_glu[l], b_glu, batch=batch, seq_len=seq_len, col0=col_s,
                                  w_c=w_c)
        xp_new = _merge(xp, mf, md, ms, w_out_b)
        st_p.append((
            z[:, w_a:2 * w_a].reshape(batch, seq_len, n_fh, hd_a),
            z[:, 2 * w_a:3 * w_a].reshape(batch, seq_len, n_fh, hd_a),
            logf[:, :n_fh].reshape(batch, seq_len, n_fh),
            z[:, col_d + w_b:col_d + 2 * w_b].reshape(batch, seq_len, n_dh, 2 * hd_b),
            z[:, col_d + 2 * w_b:col_d + 3 * w_b].reshape(batch, seq_len, n_dh, 2 * hd_b),
            hre.reshape(batch, g_c, p_state), him.reshape(batch, g_c, p_state),
        ))

        zs, logf_s, _, _ = _proj(xs, g_row, w_main, w_f, b_f, gains, tri, seq_len=n_seq * n_q, w_a=w_a, w_b=w_b)
        head_rows = lambda col, width: zs[:, col:col + width].reshape(n_seq, n_q * width // LANES, LANES)
        new_rows = ([head_rows(i * w_a, w_a) for i in range(4)]
                    + [head_rows(col_d + i * w_b, w_b) for i in range(4)])
        lfn = jnp.pad(logf_s[:, :n_fh].reshape(n_seq, 1, n_q * n_fh), ((0, 0), (0, 0), (0, LANES - n_q * n_fh)))
        mf_s, md_s = _sample_attn(page_table, new_rows, lfn, sbias, (fk_c, fv_c, dk_c, dv_c, se_c, tot_c),
                                  lam_vecs, sub_gain, layer=l, n_q=n_q, n_fh=n_fh, n_dh=n_dh, hd_b=hd_b,
                                  lambda_init=lambda_init)
        u_s = zs[:, col_s:col_s + w_c].reshape(n_seq, n_q * w_c)
        g_s = zs[:, col_s + w_c:col_s + 2 * w_c].reshape(n_seq, n_q * w_c)
        ms_s, hre_s, him_s = _s5_sample(u_s, g_s, state_s5_re[l].reshape(n_seq, n_state),
                                        state_s5_im[l].reshape(n_seq, n_state), mats, d_vec, s5_w_glu[l], b_glu,
                                        n_steps=n_q, w_c=w_c)
        xs_new = _merge(xs, mf_s.reshape(n_seq * n_q, w_a), md_s.reshape(n_seq * n_q, w_b),
                        ms_s.reshape(n_seq * n_q, w_c), w_out_b)
        st_s.append((
            zs[:, w_a:2 * w_a].reshape(n_seq, n_q, n_fh, hd_a),
            zs[:, 2 * w_a:3 * w_a].reshape(n_seq, n_q, n_fh, hd_a),
            logf_s[:, :n_fh].reshape(n_seq, n_q, n_fh),
            zs[:, col_d + w_b:col_d + 2 * w_b].reshape(n_seq, n_q, n_dh, 2 * hd_b),
            zs[:, col_d + 2 * w_b:col_d + 3 * w_b].reshape(n_seq, n_q, n_dh, 2 * hd_b),
            hre_s.reshape(n_seq, g_c, p_state), him_s.reshape(n_seq, g_c, p_state),
        ))
        xp, xs = xp_new, xs_new

    outs_p = [jnp.stack([s[i] for s in st_p]) for i in range(7)]
    outs_s = [jnp.stack([s[i] for s in st_s]) for i in range(7)]
    return (xp.reshape(batch, seq_len, d_model), xs.reshape(n_seq, n_q, d_model), *outs_p, *outs_s)
```

```python
import functools
import math

import jax
import jax.numpy as jnp
from jax import lax
from jax.experimental import pallas as pl
from jax.experimental.pallas import tpu as pltpu

F32 = jnp.float32
BF16 = jnp.bfloat16
HIGHEST = lax.Precision.HIGHEST

EPS = 1e-6
NEG = -1e30
LANES = 128
SUBLANES = 8
PAGE = 128
N_BUCKETS = 32
REL_MAX_DIST = 128
LOG2E = math.log2(math.e)
VMEM_LIMIT = 56 * 1024 * 1024

_NT = (((1,), (1,)), ((), ()))


def _params(n_axes, vmem=VMEM_LIMIT):
    return pltpu.CompilerParams(dimension_semantics=("arbitrary",) * n_axes, vmem_limit_bytes=vmem)


def _silu(x):
    return x / (1.0 + jnp.exp(-x))


def _sigmoid(x):
    return 1.0 / (1.0 + jnp.exp(-x))


def _log_sigmoid(x):
    return jnp.minimum(x, 0.0) - jnp.log1p(jnp.exp(-jnp.abs(x)))


def _gelu_tanh(x):
    return 0.5 * x * (1.0 + jnp.tanh(math.sqrt(2.0 / math.pi) * (x + 0.044715 * (x * x * x))))


CUMSUM_CHUNK = 256


def _proj_body(x_ref, g_ref, w_ref, wf_ref, bf_ref, gains_ref, tri_ref,
               z_ref, logf_ref, c_ref, ct_ref, h_scr, carry_scr, *, tm, tn, seq_tiles, seg):
    m = pl.program_id(0)
    n = pl.program_id(1)

    @pl.when(n == 0)
    def _():
        x = x_ref[...]
        ms = jnp.mean(x * x, axis=-1, keepdims=True)
        hb = (x * lax.rsqrt(ms + EPS) * g_ref[...]).astype(BF16)
        h_scr[...] = hb
        logits = jnp.dot(hb, wf_ref[...], preferred_element_type=F32) + bf_ref[...]
        lf = _log_sigmoid(logits)
        logf_ref[...] = lf

        @pl.when(m % seq_tiles == 0)
        def _():
            carry_scr[...] = jnp.zeros_like(carry_scr)

        carry = carry_scr[0:1, :]
        for j in range(tm // CUMSUM_CHUNK):
            blk = lf[j * CUMSUM_CHUNK:(j + 1) * CUMSUM_CHUNK]
            cs = jnp.dot(tri_ref[...], blk, precision=HIGHEST, preferred_element_type=F32) + carry
            c_ref[j * CUMSUM_CHUNK:(j + 1) * CUMSUM_CHUNK, :] = cs * LOG2E
            carry = cs[CUMSUM_CHUNK - 1:CUMSUM_CHUNK, :]
        carry_scr[...] = jnp.broadcast_to(carry, carry_scr.shape)
        ct_ref[...] = c_ref[...].T[0:SUBLANES, :]

    acc = jnp.dot(h_scr[...], w_ref[...], preferred_element_type=F32)
    lane = lax.broadcasted_iota(jnp.int32, (tm, LANES), 1)

    def norm_full(row):
        gain = gains_ref[row:row + 1, :]
        for j in range(tn // LANES):
            y = acc[:, j * LANES:(j + 1) * LANES]
            ms = jnp.mean(y * y, axis=-1, keepdims=True)
            z_ref[:, j * LANES:(j + 1) * LANES] = y * lax.rsqrt(ms + EPS) * gain

    def norm_half(row):
        gain = gains_ref[row:row + 1, :]
        half = LANES // 2
        for j in range(tn // LANES):
            y = acc[:, j * LANES:(j + 1) * LANES]
            y2 = y * y
            lo = jnp.sum(jnp.where(lane < half, y2, 0.0), axis=-1, keepdims=True)
            hi = jnp.sum(jnp.where(lane >= half, y2, 0.0), axis=-1, keepdims=True)
            ms = jnp.where(lane < half, lo, hi) * (1.0 / half)
            z_ref[:, j * LANES:(j + 1) * LANES] = y * lax.rsqrt(ms + EPS) * gain

    fq0, fk0, fv0, dq0, dk0, dv0 = seg

    @pl.when(n < fk0)
    def _():
        norm_full(0)

    @pl.when((n >= fk0) & (n < fv0))
    def _():
        norm_full(1)

    @pl.when(n == dq0)
    def _():
        norm_half(2)

    @pl.when(n == dk0)
    def _():
        norm_half(3)

    @pl.when(((n >= fv0) & (n < dq0)) | (n >= dv0))
    def _():
        z_ref[...] = acc


def _proj(x, g, w_main, w_f, b_f, gains, tri, *, seq_len, w_a, w_b):
    rows, d = x.shape
    n_cols = w_main.shape[1]
    tm = min(1024, rows)
    tn = 512
    assert rows % tm == 0 and n_cols % tn == 0 and seq_len % tm == 0 and tm % CUMSUM_CHUNK == 0
    assert w_a % tn == 0 and w_b == tn
    seg = (0, w_a // tn, 2 * w_a // tn, 4 * w_a // tn, 4 * w_a // tn + 1, 4 * w_a // tn + 2)
    body = functools.partial(_proj_body, tm=tm, tn=tn, seq_tiles=seq_len // tm, seg=seg)
    return pl.pallas_call(
        body,
        grid=(rows // tm, n_cols // tn),
        in_specs=[
            pl.BlockSpec((tm, d), lambda m, n: (m, 0)),
            pl.BlockSpec((1, d), lambda m, n: (0, 0)),
            pl.BlockSpec((d, tn), lambda m, n: (0, n)),
            pl.BlockSpec((d, LANES), lambda m, n: (0, 0)),
            pl.BlockSpec((1, LANES), lambda m, n: (0, 0)),
            pl.BlockSpec((4, LANES), lambda m, n: (0, 0)),
            pl.BlockSpec((CUMSUM_CHUNK, CUMSUM_CHUNK), lambda m, n: (0, 0)),
        ],
        out_specs=[
            pl.BlockSpec((tm, tn), lambda m, n: (m, n)),
            pl.BlockSpec((tm, LANES), lambda m, n: (m, 0)),
            pl.BlockSpec((tm, LANES), lambda m, n: (m, 0)),
            pl.BlockSpec((SUBLANES, tm), lambda m, n: (0, m)),
        ],
        out_shape=[
            jax.ShapeDtypeStruct((rows, n_cols), F32),
            jax.ShapeDtypeStruct((rows, LANES), F32),
            jax.ShapeDtypeStruct((rows, LANES), F32),
            jax.ShapeDtypeStruct((SUBLANES, rows), F32),
        ],
        scratch_shapes=[pltpu.VMEM((tm, d), BF16), pltpu.VMEM((SUBLANES, LANES), F32)],
        compiler_params=_params(2),
        name="proj",
    )(x, g, w_main, w_f, b_f, gains, tri)


def _fox_body(q_ref, k_ref, v_ref, c_ref, ct_ref, gate_ref, o_ref, m_scr, l_scr, acc_scr, *, n_heads, hd, tq, tk):
    qi = pl.program_id(1)
    ki = pl.program_id(2)
    nk = pl.num_programs(2)

    @pl.when(ki == 0)
    def _():
        m_scr[...] = jnp.full_like(m_scr, NEG)
        l_scr[...] = jnp.zeros_like(l_scr)
        acc_scr[...] = jnp.zeros_like(acc_scr)

    def step(diagonal):
        if diagonal:
            visible = (lax.broadcasted_iota(jnp.int32, (tq, tk), 1) <= lax.broadcasted_iota(jnp.int32, (tq, tk), 0))
        def scores(h):
            sl = slice(h * hd, (h + 1) * hd)
            q = q_ref[:, sl].astype(BF16)
            k = k_ref[:, sl].astype(BF16)
            s = lax.dot_general(q, k, _NT, preferred_element_type=F32) - ct_ref[h:h + 1, :]
            return jnp.where(visible, s, NEG) if diagonal else s

        s_next = scores(0)
        for h in range(n_heads):
            sl = slice(h * hd, (h + 1) * hd)
            s = s_next
            if h + 1 < n_heads:
                s_next = scores(h + 1)
            cq = c_ref[:, h:h + 1]
            m_prev = m_scr[h]
            m_new = jnp.maximum(m_prev, jnp.max(s, axis=-1, keepdims=True) + cq)
            alpha = jnp.exp2(m_prev - m_new)
            p = jnp.exp2(s + (cq - m_new[:, 0:1]))
            l_scr[h] = alpha * l_scr[h] + jnp.sum(p, axis=-1, keepdims=True)
            pv = jnp.dot(p.astype(BF16), v_ref[:, sl].astype(BF16), preferred_element_type=F32)
            acc_scr[:, sl] = alpha * acc_scr[:, sl] + pv
            m_scr[h] = m_new

    @pl.when(ki < qi)
    def _():
        step(False)

    @pl.when(ki == qi)
    def _():
        step(True)

    @pl.when(ki == nk - 1)
    def _():
        for h in range(n_heads):
            sl = slice(h * hd, (h + 1) * hd)
            o = acc_scr[:, sl] / l_scr[h]
            o_ref[:, sl] = (o * _silu(gate_ref[:, sl])).astype(BF16)


def _fox_prompt(z, c, ct, *, batch, seq_len, n_heads, hd):
    w = n_heads * hd
    tq = tk = min(512, seq_len)
    nq = seq_len // tq
    body = functools.partial(_fox_body, n_heads=n_heads, hd=hd, tq=tq, tk=tk)
    kv_row = lambda b, qi, ki: b * nq + jnp.minimum(ki, qi)
    return pl.pallas_call(
        body,
        grid=(batch, nq, nq),
        in_specs=[
            pl.BlockSpec((tq, w), lambda b, qi, ki: (b * nq + qi, 0)),
            pl.BlockSpec((tk, w), lambda b, qi, ki: (kv_row(b, qi, ki), 1)),
            pl.BlockSpec((tk, w), lambda b, qi, ki: (kv_row(b, qi, ki), 2)),
            pl.BlockSpec((tq, LANES), lambda b, qi, ki: (b * nq + qi, 0)),
            pl.BlockSpec((SUBLANES, tk), lambda b, qi, ki: (0, kv_row(b, qi, ki))),
            pl.BlockSpec((tq, w), lambda b, qi, ki: (b * nq + qi, 3)),
        ],
        out_specs=pl.BlockSpec((tq, w), lambda b, qi, ki: (b * nq + qi, 0)),
        out_shape=jax.ShapeDtypeStruct((batch * seq_len, w), BF16),
        scratch_shapes=[
            pltpu.VMEM((n_heads, tq, LANES), F32),
            pltpu.VMEM((n_heads, tq, LANES), F32),
            pltpu.VMEM((tq, w), F32),
        ],
        compiler_params=_params(3),
        name="fox_prompt",
    )(z, z, z, c, ct, z)


def _rel_bucket(d):
    max_exact = N_BUCKETS // 2
    df = jnp.maximum(d, 1).astype(F32)
    large = max_exact + (jnp.log(df / max_exact) / math.log(REL_MAX_DIST / max_exact)
                         * (N_BUCKETS - max_exact)).astype(jnp.int32)
    large = jnp.minimum(large, N_BUCKETS - 1)
    return jnp.where(d < max_exact, d, large)


def _bias_lookup(rel_ref, d, h):
    bucket = _rel_bucket(jnp.maximum(d, 0))
    val = jnp.full(d.shape, rel_ref[0, h], F32)
    for b in range(1, N_BUCKETS):
        val = jnp.where(bucket == b, rel_ref[b, h], val)
    return jnp.where(d >= 0, val * LOG2E, NEG)


def _bias_tiles_body(rel_ref, o_ref, *, tq, tk):
    h = pl.program_id(0)
    kind = pl.program_id(1)
    row = lax.broadcasted_iota(jnp.int32, (tq, tk), 0)
    col = lax.broadcasted_iota(jnp.int32, (tq, tk), 1)
    o_ref[...] = _bias_lookup(rel_ref, kind * tq + row - col, h)


def _bias_tiles(rel_bias, *, n_heads, tq, tk):
    return pl.pallas_call(
        functools.partial(_bias_tiles_body, tq=tq, tk=tk),
        grid=(n_heads, 3),
        in_specs=[pl.BlockSpec(memory_space=pltpu.SMEM)],
        out_specs=pl.BlockSpec((None, None, tq, tk), lambda h, kind: (h, kind, 0, 0)),
        out_shape=jax.ShapeDtypeStruct((n_heads, 3, tq, tk), F32),
        compiler_params=_params(2),
        name="bias_tiles",
    )(rel_bias)


def _sample_bias_body(rel_ref, o_ref, *, n_q, n_heads):
    kind = pl.program_id(0)
    rows = 2 * n_q * n_heads
    width = PAGE * n_heads
    r = lax.broadcasted_iota(jnp.int32, (rows, width), 0)
    lane = lax.broadcasted_iota(jnp.int32, (rows, width), 1)
    i = (r % (n_q * n_heads)) // n_heads
    pos = lane // n_heads
    d_last = PAGE + i - pos
    d_new = jnp.where(pos < n_q, i - pos, -1)
    d_far = jnp.full((rows, width), 2 * PAGE, jnp.int32)
    d = jnp.where(kind == 0, d_last, jnp.where(kind == 1, d_new, d_far))
    out = jnp.full((rows, width), NEG, F32)
    for h in range(n_heads):
        own = (lane % n_heads == h) & (r % n_heads == h)
        out = jnp.where(own, _bias_lookup(rel_ref, d, h), out)
    o_ref[...] = out


def _sample_bias(rel_bias, *, n_heads, n_q):
    rows = 2 * n_q * n_heads
    width = PAGE * n_heads
    return pl.pallas_call(
        functools.partial(_sample_bias_body, n_q=n_q, n_heads=n_heads),
        grid=(3,),
        in_specs=[pl.BlockSpec(memory_space=pltpu.SMEM)],
        out_specs=pl.BlockSpec((None, rows, width), lambda kind: (kind, 0, 0)),
        out_shape=jax.ShapeDtypeStruct((3, rows, width), F32),
        compiler_params=_params(1),
        name="sample_bias",
    )(rel_bias)


def _diff_lambda(lam_ref, lambda_init):
    lv = lam_ref[...]
    e1 = jnp.exp(jnp.sum(lv[0:1, :] * lv[1:2, :], axis=-1, keepdims=True))
    e2 = jnp.exp(jnp.sum(lv[2:3, :] * lv[3:4, :], axis=-1, keepdims=True))
    return e1 - e2 + lambda_init


def _diff_body(q_ref, k_ref, v_ref, bias_ref, gate_ref, lam_ref, sub_ref, o_ref,
               m_scr, l_scr, acc_scr, *, n_heads, hd, tq, tk, lambda_init):
    qi = pl.program_id(1)
    ki = pl.program_id(2)
    nk = pl.num_programs(2)
    hw = 2 * hd

    @pl.when(ki == 0)
    def _():
        m_scr[...] = jnp.full_like(m_scr, NEG)
        l_scr[...] = jnp.zeros_like(l_scr)
        acc_scr[...] = jnp.zeros_like(acc_scr)

    @pl.when(ki <= qi)
    def _():
        lane = lax.broadcasted_iota(jnp.int32, (tq, hw), 1)

        def scores(idx):
            h, mp = divmod(idx, 2)
            sl = slice(h * hw, (h + 1) * hw)
            qm = jnp.where((lane < hd) if mp == 0 else (lane >= hd), q_ref[:, sl], 0.0).astype(BF16)
            k = k_ref[:, sl].astype(BF16)
            return lax.dot_general(qm, k, _NT, preferred_element_type=F32) + bias_ref[h]

        s_next = scores(0)
        for idx in range(2 * n_heads):
            h = idx // 2
            s = s_next
            if idx + 1 < 2 * n_heads:
                s_next = scores(idx + 1)
            m_prev = m_scr[idx]
            m_new = jnp.maximum(m_prev, jnp.max(s, axis=-1, keepdims=True))
            alpha = jnp.exp2(m_prev - m_new)
            p = jnp.exp2(s - m_new[:, 0:1])
            l_scr[idx] = alpha * l_scr[idx] + jnp.sum(p, axis=-1, keepdims=True)
            v = v_ref[:, h * hw:(h + 1) * hw].astype(BF16)
            pv = jnp.dot(p.astype(BF16), v, preferred_element_type=F32)
            acc_scr[idx] = alpha * acc_scr[idx] + pv
            m_scr[idx] = m_new

    @pl.when(ki == nk - 1)
    def _():
        lam = _diff_lambda(lam_ref, lambda_init)
        for h in range(n_heads):
            sl = slice(h * hw, (h + 1) * hw)
            o = acc_scr[2 * h] / l_scr[2 * h] - lam * (acc_scr[2 * h + 1] / l_scr[2 * h + 1])
            ms = jnp.mean(o * o, axis=-1, keepdims=True)
            o = o * lax.rsqrt(ms + EPS) * sub_ref[...] * (1.0 - lambda_init)
            o_ref[:, sl] = (o * _silu(gate_ref[:, sl])).astype(BF16)


def _diff_prompt(z, bias_tiles, lam_vecs, sub_gain, *, batch, seq_len, n_heads, hd, col0, lambda_init):
    hw = 2 * hd
    w = n_heads * hw
    tq = tk = min(512, seq_len)
    nq = seq_len // tq
    cb = col0 // w
    body = functools.partial(_diff_body, n_heads=n_heads, hd=hd, tq=tq, tk=tk, lambda_init=lambda_init)
    kv_row = lambda b, qi, ki: b * nq + jnp.minimum(ki, qi)
    kind = lambda b, qi, ki: jnp.clip(qi - jnp.minimum(ki, qi), 0, 2)
    return pl.pallas_call(
        body,
        grid=(batch, nq, nq),
        in_specs=[
            pl.BlockSpec((tq, w), lambda b, qi, ki: (b * nq + qi, cb)),
            pl.BlockSpec((tk, w), lambda b, qi, ki: (kv_row(b, qi, ki), cb + 1)),
            pl.BlockSpec((tk, w), lambda b, qi, ki: (kv_row(b, qi, ki), cb + 2)),
            pl.BlockSpec((n_heads, None, tq, tk), lambda b, qi, ki: (0, kind(b, qi, ki), 0, 0)),
            pl.BlockSpec((tq, w), lambda b, qi, ki: (b * nq + qi, cb + 3)),
            pl.BlockSpec((4, hd), lambda b, qi, ki: (0, 0)),
            pl.BlockSpec((1, hw), lambda b, qi, ki: (0, 0)),
        ],
        out_specs=pl.BlockSpec((tq, w), lambda b, qi, ki: (b * nq + qi, 0)),
        out_shape=jax.ShapeDtypeStruct((batch * seq_len, w), BF16),
        scratch_shapes=[
            pltpu.VMEM((2 * n_heads, tq, LANES), F32),
            pltpu.VMEM((2 * n_heads, tq, LANES), F32),
            pltpu.VMEM((2 * n_heads, tq, hw), F32),
        ],
        compiler_params=_params(3),
        name="diff_prompt",
    )(z, z, z, bias_tiles, z, lam_vecs, sub_gain)


def _s5_prep_body(lre_ref, lim_ref, ls_ref, bre_ref, bim_ref, are_ref, aim_ref, bbre_ref, bbim_ref):
    lr = lre_ref[...]
    li = lim_ref[...]
    step = jnp.exp(ls_ref[...])
    mag = jnp.exp(lr * step)
    ar = mag * jnp.cos(li * step)
    ai = mag * jnp.sin(li * step)
    den = lr * lr + li * li
    nr = ar - 1.0
    cr = (nr * lr + ai * li) / den
    ci = (ai * lr - nr * li) / den
    br = bre_ref[...]
    bi = bim_ref[...]
    are_ref[...] = ar
    aim_ref[...] = ai
    bbre_ref[...] = cr * br - ci * bi
    bbim_ref[...] = cr * bi + ci * br


def _s5_prep(lam_re, lam_im, log_step, b_re, b_im):
    g, p, c = b_re.shape
    col = lambda a: a.reshape(g, p, 1)
    ls = jnp.broadcast_to(log_step.reshape(g, 1, 1), (g, p, 1))
    full3 = lambda shp: pl.BlockSpec(shp, lambda: (0, 0, 0))
    return pl.pallas_call(
        _s5_prep_body,
        in_specs=[full3((g, p, 1))] * 3 + [full3((g, p, c))] * 2,
        out_specs=[full3((g, p, 1))] * 2 + [full3((g, p, c))] * 2,
        out_shape=[jax.ShapeDtypeStruct((g, p, 1), F32)] * 2 + [jax.ShapeDtypeStruct((g, p, c), F32)] * 2,
        name="s5_prep",
    )(col(lam_re), col(lam_im), ls, b_re, b_im)


def _block_diag(blocks):
    g, r, c = blocks.shape
    eye = jnp.eye(g, dtype=blocks.dtype)
    return (blocks[:, :, None, :] * eye[:, None, :, None]).reshape(g * r, g * c)


S5_LANE_CHUNK = 256


def _s5_scan_body(u_ref, gate_ref, bre_ref, bim_ref, cre_ref, cim_ref, are_ref, aim_ref, d_ref, wg_ref, bg_ref,
                  o_ref, hre_ref, him_ref, pwr_scr, pwi_scr, hr_scr, hi_scr, cr_scr, ci_scr, *, tm, n_state):
    b = pl.program_id(0)
    i = pl.program_id(1)
    nt = pl.num_programs(1)
    n_levels = tm.bit_length() - 1

    @pl.when((b == 0) & (i == 0))
    def _():
        pwr_scr[0:1, :] = are_ref[...]
        pwi_scr[0:1, :] = aim_ref[...]
        for lv in range(n_levels):
            s = 1 << lv
            tr = pwr_scr[s - 1:s, :]
            ti = pwi_scr[s - 1:s, :]
            xr = pwr_scr[0:s, :]
            xi = pwi_scr[0:s, :]
            pwr_scr[s:2 * s, :] = xr * tr - xi * ti
            pwi_scr[s:2 * s, :] = xr * ti + xi * tr

    @pl.when(i == 0)
    def _():
        cr_scr[...] = jnp.zeros_like(cr_scr)
        ci_scr[...] = jnp.zeros_like(ci_scr)

    u = u_ref[...]
    ub = u.astype(BF16)
    hr_scr[...] = jnp.dot(ub, bre_ref[...], preferred_element_type=F32)
    hi_scr[...] = jnp.dot(ub, bim_ref[...], preferred_element_type=F32)

    row = lax.broadcasted_iota(jnp.int32, (tm, S5_LANE_CHUNK), 0)

    def chunk(j, carry):
        ln = pl.ds(pl.multiple_of(j * S5_LANE_CHUNK, S5_LANE_CHUNK), S5_LANE_CHUNK)
        hr = hr_scr[:, ln]
        hi = hi_scr[:, ln]
        for lv in range(n_levels):
            s = 1 << lv
            mr = pwr_scr[s - 1:s, ln]
            mi = pwi_scr[s - 1:s, ln]
            sr = jnp.where(row >= s, pltpu.roll(hr, s, 0), 0.0)
            si = jnp.where(row >= s, pltpu.roll(hi, s, 0), 0.0)
            hr, hi = hr + (mr * sr - mi * si), hi + (mr * si + mi * sr)
        pr = cr_scr[0:1, ln]
        pi = ci_scr[0:1, ln]
        wr = pwr_scr[:, ln]
        wi = pwi_scr[:, ln]
        hr, hi = hr + (wr * pr - wi * pi), hi + (wr * pi + wi * pr)
        hr_scr[:, ln] = hr
        hi_scr[:, ln] = hi
        cr_scr[:, ln] = jnp.broadcast_to(hr[tm - 1:tm, :], (SUBLANES, S5_LANE_CHUNK))
        ci_scr[:, ln] = jnp.broadcast_to(hi[tm - 1:tm, :], (SUBLANES, S5_LANE_CHUNK))
        return carry

    lax.fori_loop(0, n_state // S5_LANE_CHUNK, chunk, 0)

    y = (jnp.dot(hr_scr[...].astype(BF16), cre_ref[...], preferred_element_type=F32)
         - jnp.dot(hi_scr[...].astype(BF16), cim_ref[...], preferred_element_type=F32)
         + d_ref[...] * u)
    zz = _gelu_tanh(y)
    glu = zz * _sigmoid(jnp.dot(zz.astype(BF16), wg_ref[...], preferred_element_type=F32) + bg_ref[...])
    o_ref[...] = (glu * _silu(gate_ref[...])).astype(BF16)

    @pl.when(i == nt - 1)
    def _():
        hre_ref[...] = cr_scr[0:1, :]
        him_ref[...] = ci_scr[0:1, :]


def _s5_prompt(z, mats, d_vec, w_glu, b_glu, *, batch, seq_len, col0, w_c):
    bre, bim, cre, cim, are, aim = mats
    n_state = bre.shape[1]
    tm = min(512, seq_len)
    nt = seq_len // tm
    cb = col0 // w_c
    body = functools.partial(_s5_scan_body, tm=tm, n_state=n_state)
    const = lambda shp: pl.BlockSpec(shp, lambda b, i: (0,) * len(shp))
    return pl.pallas_call(
        body,
        grid=(batch, nt),
        in_specs=[
            pl.BlockSpec((tm, w_c), lambda b, i: (b * nt + i, cb)),
            pl.BlockSpec((tm, w_c), lambda b, i: (b * nt + i, cb + 1)),
            const((w_c, n_state)), const((w_c, n_state)),
            const((n_state, w_c)), const((n_state, w_c)),
            const((1, n_state)), const((1, n_state)),
            const((1, w_c)), const((w_c, w_c)), const((1, w_c)),
        ],
        out_specs=[
            pl.BlockSpec((tm, w_c), lambda b, i: (b * nt + i, 0)),
            pl.BlockSpec((None, 1, n_state), lambda b, i: (b, 0, 0)),
            pl.BlockSpec((None, 1, n_state), lambda b, i: (b, 0, 0)),
        ],
        out_shape=[
            jax.ShapeDtypeStruct((batch * seq_len, w_c), BF16),
            jax.ShapeDtypeStruct((batch, 1, n_state), F32),
            jax.ShapeDtypeStruct((batch, 1, n_state), F32),
        ],
        scratch_shapes=[
            pltpu.VMEM((tm, n_state), F32), pltpu.VMEM((tm, n_state), F32),
            pltpu.VMEM((tm, n_state), F32), pltpu.VMEM((tm, n_state), F32),
            pltpu.VMEM((SUBLANES, n_state), F32), pltpu.VMEM((SUBLANES, n_state), F32),
        ],
        compiler_params=_params(2),
        name="s5_prompt",
    )(z, z, bre.astype(BF16), bim.astype(BF16), cre.astype(BF16), cim.astype(BF16), are, aim,
      d_vec, w_glu.astype(BF16), b_glu)


def _s5_sample_body(u_ref, gate_ref, h0r_ref, h0i_ref, bre_ref, bim_ref, cre_ref, cim_ref, are_ref, aim_ref,
                    d_ref, wg_ref, bg_ref, o_ref, hre_ref, him_ref, *, n_steps, w_c):
    hr = h0r_ref[...]
    hi = h0i_ref[...]
    ar = are_ref[...]
    ai = aim_ref[...]
    for t in range(n_steps):
        sl = slice(t * w_c, (t + 1) * w_c)
        u = u_ref[:, sl]
        bur = jnp.dot(u, bre_ref[...], precision=HIGHEST, preferred_element_type=F32)
        bui = jnp.dot(u, bim_ref[...], precision=HIGHEST, preferred_element_type=F32)
        hr, hi = ar * hr - ai * hi + bur, ar * hi + ai * hr + bui
        y = (jnp.dot(hr.astype(BF16), cre_ref[...], preferred_element_type=F32)
             - jnp.dot(hi.astype(BF16), cim_ref[...], preferred_element_type=F32)
             + d_ref[...] * u)
        zz = _gelu_tanh(y)
        glu = zz * _sigmoid(jnp.dot(zz.astype(BF16), wg_ref[...], preferred_element_type=F32) + bg_ref[...])
        o_ref[:, sl] = (glu * _silu(gate_ref[:, sl])).astype(BF16)
    hre_ref[...] = hr
    him_ref[...] = hi


def _s5_sample(u, gate, h0r, h0i, mats, d_vec, w_glu, b_glu, *, n_steps, w_c):
    bre, bim, cre, cim, are, aim = mats
    n_seq = u.shape[0]
    n_state = bre.shape[1]
    body = functools.partial(_s5_sample_body, n_steps=n_steps, w_c=w_c)
    return pl.pallas_call(
        body,
        out_shape=[
            jax.ShapeDtypeStruct((n_seq, n_steps * w_c), BF16),
            jax.ShapeDtypeStruct((n_seq, n_state), F32),
            jax.ShapeDtypeStruct((n_seq, n_state), F32),
        ],
        compiler_params=pltpu.CompilerParams(vmem_limit_bytes=VMEM_LIMIT),
        name="s5_sample",
    )(u, gate, h0r, h0i, bre, bim, cre.astype(BF16), cim.astype(BF16), are, aim, d_vec, w_glu.astype(BF16), b_glu)


def _merge_body(x_ref, mf_ref, md_ref, ms_ref, w_ref, o_ref, *, w_a, w_b):
    acc = jnp.dot(mf_ref[...].astype(BF16), w_ref[0:w_a, :], preferred_element_type=F32)
    acc += jnp.dot(md_ref[...].astype(BF16), w_ref[w_a:w_a + w_b, :], preferred_element_type=F32)
    acc += jnp.dot(ms_ref[...].astype(BF16), w_ref[w_a + w_b:, :], preferred_element_type=F32)
    o_ref[...] = x_ref[...] + acc


def _merge(x, mf, md, ms, w_out_bf16):
    rows, d = x.shape
    w_a, w_b, w_c = mf.shape[1], md.shape[1], ms.shape[1]
    tm = min(512, rows)
    body = functools.partial(_merge_body, w_a=w_a, w_b=w_b)
    return pl.pallas_call(
        body,
        grid=(rows // tm,),
        in_specs=[
            pl.BlockSpec((tm, d), lambda m: (m, 0)),
            pl.BlockSpec((tm, w_a), lambda m: (m, 0)),
            pl.BlockSpec((tm, w_b), lambda m: (m, 0)),
            pl.BlockSpec((tm, w_c), lambda m: (m, 0)),
            pl.BlockSpec((w_a + w_b + w_c, d), lambda m: (0, 0)),
        ],
        out_specs=pl.BlockSpec((tm, d), lambda m: (m, 0)),
        out_shape=jax.ShapeDtypeStruct((rows, d), F32),
        compiler_params=_params(1),
        name="merge",
    )(x, mf, md, ms, w_out_bf16)


def _logf_suffix_body(x_ref, se_ref, tot_ref, *, n_heads):
    x = x_ref[...]
    width = x.shape[1]
    lane = lax.broadcasted_iota(jnp.int32, x.shape, 1)
    si = x
    tot = x
    s = n_heads
    while s < width:
        si = si + jnp.where(lane < width - s, pltpu.roll(si, width - s, 1), 0.0)
        tot = tot + pltpu.roll(tot, s, 1)
        s *= 2
    se_ref[...] = (si - x) * LOG2E
    tot_ref[...] = tot * LOG2E


def _logf_suffix(logf_pages, *, n_heads):
    rows, width = logf_pages.shape
    tr = max(t for t in range(SUBLANES, 256 + 1, SUBLANES) if rows % t == 0)
    return pl.pallas_call(
        functools.partial(_logf_suffix_body, n_heads=n_heads),
        grid=(rows // tr,),
        in_specs=[pl.BlockSpec((tr, width), lambda r: (r, 0))],
        out_specs=[pl.BlockSpec((tr, width), lambda r: (r, 0))] * 2,
        out_shape=[jax.ShapeDtypeStruct((rows, width), F32)] * 2,
        compiler_params=_params(1),
        name="logf_suffix",
    )(logf_pages)


def _softmax_block(scores, values, m_scr, l_scr, acc_scr):
    m_prev = m_scr[...]
    m_cur = functools.reduce(jnp.maximum, [jnp.max(s, axis=-1, keepdims=True) for s in scores])
    m_new = jnp.maximum(m_prev, m_cur)
    alpha = jnp.exp2(m_prev - m_new)
    l_new = alpha * l_scr[...]
    acc = alpha * acc_scr[...]
    for s, v in zip(scores, values):
        p = jnp.exp2(s - m_new[:, 0:1])
        l_new = l_new + jnp.sum(p, axis=-1, keepdims=True)
        acc = acc + jnp.dot(p, v, preferred_element_type=F32)
    m_scr[...] = m_new
    l_scr[...] = l_new
    acc_scr[...] = acc


def _sample_attn_body(pt_ref, *refs, n_q, pps, n_fh, n_dh, hd_b, lambda_init):
    del pt_ref
    n_in = 10 + 6 * pps
    (fq_ref, fkn_ref, fvn_ref, fg_ref, dq_ref, dkn_ref, dvn_ref, dg_ref, lfn_ref, sb_ref) = refs[:10]
    page_refs = refs[10:n_in]
    lam_ref, sub_ref, of_ref, od_ref = refs[n_in:n_in + 4]
    (qd_scr, cq_scr, r_scr, fkn_scr, fvn_scr, dkn_scr, dvn_scr,
     fm_scr, fl_scr, facc_scr, dm_scr, dl_scr, dacc_scr) = refs[n_in + 4:]
    j = pl.program_id(1)
    nj = pl.num_programs(1)
    rf = n_q * n_fh
    rd = n_q * n_dh
    wf = PAGE * n_fh

    @pl.when(j == 0)
    def _():
        fm_scr[...] = jnp.full_like(fm_scr, NEG)
        fl_scr[...] = jnp.zeros_like(fl_scr)
        facc_scr[...] = jnp.zeros_like(facc_scr)
        dm_scr[...] = jnp.full_like(dm_scr, NEG)
        dl_scr[...] = jnp.zeros_like(dl_scr)
        dacc_scr[...] = jnp.zeros_like(dacc_scr)
        r_scr[...] = jnp.zeros_like(r_scr)
        dq = dq_ref[...]
        dlane = lax.broadcasted_iota(jnp.int32, (rd, LANES), 1)
        qd_scr[0:rd, :] = jnp.where(dlane < hd_b, dq, 0.0)
        qd_scr[rd:2 * rd, :] = jnp.where(dlane >= hd_b, dq, 0.0)
        fkn_scr[...] = jnp.zeros_like(fkn_scr)
        fvn_scr[...] = jnp.zeros_like(fvn_scr)
        dkn_scr[...] = jnp.zeros_like(dkn_scr)
        dvn_scr[...] = jnp.zeros_like(dvn_scr)
        fkn_scr[0:rf, :] = fkn_ref[...]
        fvn_scr[0:rf, :] = fvn_ref[...]
        dkn_scr[0:rd, :] = dkn_ref[...]
        dvn_scr[0:rd, :] = dvn_ref[...]
        ln = lax.broadcasted_iota(jnp.int32, (1, LANES), 1)
        cn = lfn_ref[...]
        s = n_fh
        while s < rf:
            cn = cn + jnp.where(ln >= s, pltpu.roll(cn, s, 1), 0.0)
            s *= 2
        cn = cn * LOG2E
        rr = lax.broadcasted_iota(jnp.int32, (rf, LANES), 0)
        ll = lax.broadcasted_iota(jnp.int32, (rf, LANES), 1)
        cq = jnp.sum(jnp.where(ll == rr, cn, 0.0), axis=-1, keepdims=True)
        cq_scr[...] = jnp.broadcast_to(cq, (rf, LANES))
        ok = (ll < rf) & (ll % n_fh == rr % n_fh) & (ll // n_fh <= rr // n_fh)
        sf = lax.dot_general(fq_ref[...], fkn_scr[...], _NT, preferred_element_type=F32)
        sf = jnp.where(ok, sf + cq - cn, NEG)
        _softmax_block([sf], [fvn_scr[...]], fm_scr, fl_scr, facc_scr)
        sd = lax.dot_general(qd_scr[...], dkn_scr[...], _NT, preferred_element_type=F32)
        sd = sd + sb_ref[1][:, 0:LANES]
        _softmax_block([sd], [dvn_scr[...]], dm_scr, dl_scr, dacc_scr)

    frow = lax.broadcasted_iota(jnp.int32, (rf, wf), 0)
    flane = lax.broadcasted_iota(jnp.int32, (rf, wf), 1)
    own = (flane % n_fh) == (frow % n_fh)
    cq = cq_scr[:, 0:1]
    fq = fq_ref[...]
    qd = qd_scr[...]
    rsum = r_scr[...]
    first_kind = jnp.where(j == 0, 0, 2)
    f_scores, f_values, d_scores, d_values = [], [], [], []
    for g in range(pps):
        fk_ref, fv_ref, dk_ref, dv_ref, se_ref, tot_ref = page_refs[6 * g:6 * g + 6]
        sf = lax.dot_general(fq, fk_ref[...], _NT, preferred_element_type=F32)
        f_scores.append(jnp.where(own, sf + cq + (se_ref[...] + rsum), NEG))
        f_values.append(fv_ref[...])
        rsum = rsum + tot_ref[...]
        sbias = sb_ref[first_kind] if g == 0 else sb_ref[2]
        sd = lax.dot_general(qd, dk_ref[...], _NT, preferred_element_type=F32)
        d_scores.append(sd + sbias)
        d_values.append(dv_ref[...])
    r_scr[...] = rsum
    _softmax_block(f_scores, f_values, fm_scr, fl_scr, facc_scr)
    _softmax_block(d_scores, d_values, dm_scr, dl_scr, dacc_scr)

    @pl.when(j == nj - 1)
    def _():
        of_ref[...] = (facc_scr[...] / fl_scr[...]) * _silu(fg_ref[...])
        lam = _diff_lambda(lam_ref, lambda_init)
        o = dacc_scr[...] / dl_scr[...]
        od = o[0:rd, :] - lam * o[rd:2 * rd, :]
        ms = jnp.mean(od * od, axis=-1, keepdims=True)
        od = od * lax.rsqrt(ms + EPS) * sub_ref[...] * (1.0 - lambda_init)
        od_ref[...] = od * _silu(dg_ref[...])


def _sample_attn(page_table, new_rows, lfn, sbias, caches, lam_vecs, sub_gain, *, layer, n_q, n_fh, n_dh, hd_b,
                 lambda_init):
    fk_c, fv_c, dk_c, dv_c, se_c, tot_c = caches
    fq, fkn, fvn, fg, dq, dkn, dvn, dg = new_rows
    n_seq = fq.shape[0]
    n_pages = page_table.shape[1]
    pps = max(p for p in (8, 4, 2, 1) if n_pages % p == 0)
    nj = n_pages // pps
    rf = n_q * n_fh
    rd = n_q * n_dh
    wf = PAGE * n_fh
    wd = PAGE * n_dh
    body = functools.partial(_sample_attn_body, n_q=n_q, pps=pps, n_fh=n_fh, n_dh=n_dh, hd_b=hd_b,
                             lambda_init=lambda_init)
    seq_spec = lambda r: pl.BlockSpec((None, r, LANES), lambda s, j, pt: (s, 0, 0))

    def page_specs(g):
        idx = lambda s, j, pt: (layer, pt[s, n_pages - 1 - (j * pps + g)], 0, 0)
        return [
            pl.BlockSpec((None, None, wf, LANES), idx), pl.BlockSpec((None, None, wf, LANES), idx),
            pl.BlockSpec((None, None, wd, LANES), idx), pl.BlockSpec((None, None, wd, LANES), idx),
            pl.BlockSpec((None, None, 1, wf), idx), pl.BlockSpec((None, None, 1, wf), idx),
        ]

    in_specs = [seq_spec(rf)] * 4 + [seq_spec(rd)] * 4 + [
        seq_spec(1),
        pl.BlockSpec(sbias.shape, lambda s, j, pt: (0, 0, 0)),
    ]
    args = [fq, fkn, fvn, fg, dq, dkn, dvn, dg, lfn, sbias]
    for g in range(pps):
        in_specs += page_specs(g)
        args += [fk_c, fv_c, dk_c, dv_c, se_c, tot_c]
    in_specs += [pl.BlockSpec((4, hd_b), lambda s, j, pt: (0, 0)), pl.BlockSpec((1, 2 * hd_b), lambda s, j, pt: (0, 0))]
    args += [lam_vecs, sub_gain]
    grid_spec = pltpu.PrefetchScalarGridSpec(
        num_scalar_prefetch=1,
        grid=(n_seq, nj),
        in_specs=in_specs,
        out_specs=[seq_spec(rf), seq_spec(rd)],
        scratch_shapes=[
            pltpu.VMEM((2 * rd, LANES), F32),
            pltpu.VMEM((rf, LANES), F32),
            pltpu.VMEM((1, wf), F32),
            pltpu.VMEM((PAGE, LANES), F32), pltpu.VMEM((PAGE, LANES), F32),
            pltpu.VMEM((PAGE, LANES), F32), pltpu.VMEM((PAGE, LANES), F32),
            pltpu.VMEM((rf, LANES), F32), pltpu.VMEM((rf, LANES), F32), pltpu.VMEM((rf, LANES), F32),
            pltpu.VMEM((2 * rd, LANES), F32), pltpu.VMEM((2 * rd, LANES), F32), pltpu.VMEM((2 * rd, LANES), F32),
        ],
    )
    return pl.pallas_call(
        body,
        grid_spec=grid_spec,
        out_shape=[
            jax.ShapeDtypeStruct((n_seq, rf, LANES), F32),
            jax.ShapeDtypeStruct((n_seq, rd, LANES), F32),
        ],
        compiler_params=_params(2),
        name="sample_attn",
    )(page_table, *args)


def kernel(x_prompt, x_sample, cache_fox_k, cache_fox_v, cache_fox_logf, cache_diff_k, cache_diff_v, state_s5_re, state_s5_im, page_table, norm_gain, w_in, fox_b_f, fox_q_gain, fox_k_gain, diff_q_gain, diff_k_gain, diff_lambda_q1, diff_lambda_k1, diff_lambda_q2, diff_lambda_k2, diff_subln_gain, rel_bias, s5_lambda_re, s5_lambda_im, s5_b_re, s5_b_im, s5_c_re, s5_c_im, s5_d, s5_log_step, s5_w_glu, s5_b_glu, w_out):
    batch, seq_len, d_model = x_prompt.shape
    n_seq, n_q, _ = x_sample.shape
    depth, n_pool, page, n_fh, hd_a = cache_fox_k.shape
    n_dh = cache_diff_k.shape[3]
    hd_b = cache_diff_k.shape[4] // 2
    g_c, p_state, c_group = s5_b_re.shape[1:]
    assert page == PAGE and n_fh == SUBLANES and hd_a == LANES and 2 * hd_b == LANES
    w_a = n_fh * hd_a
    w_b = n_dh * 2 * hd_b
    w_c = g_c * c_group
    n_state = g_c * p_state
    col_d = 4 * w_a
    col_s = col_d + 4 * w_b

    xp = x_prompt.reshape(batch * seq_len, d_model)
    xs = x_sample.reshape(n_seq * n_q, d_model)

    fk_c = cache_fox_k.reshape(depth, n_pool, PAGE * n_fh, hd_a)
    fv_c = cache_fox_v.reshape(depth, n_pool, PAGE * n_fh, hd_a)
    dk_c = cache_diff_k.reshape(depth, n_pool, PAGE * n_dh, 2 * hd_b)
    dv_c = cache_diff_v.reshape(depth, n_pool, PAGE * n_dh, 2 * hd_b)
    se_c, tot_c = _logf_suffix(cache_fox_logf.reshape(depth * n_pool, PAGE * n_fh), n_heads=n_fh)
    se_c = se_c.reshape(depth, n_pool, 1, PAGE * n_fh)
    tot_c = tot_c.reshape(depth, n_pool, 1, PAGE * n_fh)

    tile = min(512, seq_len)
    bias_tiles = _bias_tiles(rel_bias, n_heads=n_dh, tq=tile, tk=tile)
    sbias = _sample_bias(rel_bias, n_heads=n_dh, n_q=n_q)
    tri = jnp.tril(jnp.ones((CUMSUM_CHUNK, CUMSUM_CHUNK), F32))

    st_p, st_s = [], []
    for l in range(depth):
        lambda_init = 0.8 - 0.6 * math.exp(-0.3 * l)
        w = w_in[l]
        w_main = jnp.concatenate([w[:, :3 * w_a], w[:, 3 * w_a + n_fh:]], axis=1).astype(BF16)
        w_f = jnp.pad(w[:, 3 * w_a:3 * w_a + n_fh], ((0, 0), (0, LANES - n_fh))).astype(BF16)
        b_f = jnp.pad(fox_b_f[l], (0, LANES - n_fh)).reshape(1, LANES)
        gains = jnp.stack([fox_q_gain[l] * (hd_a ** -0.5 * LOG2E), fox_k_gain[l],
                           jnp.tile(diff_q_gain[l], 2) * (hd_b ** -0.5 * LOG2E), jnp.tile(diff_k_gain[l], 2)])
        g_row = norm_gain[l].reshape(1, d_model)
        w_out_b = w_out[l].astype(BF16)
        lam_vecs = jnp.stack([diff_lambda_q1[l], diff_lambda_k1[l], diff_lambda_q2[l], diff_lambda_k2[l]])
        sub_gain = diff_subln_gain[l].reshape(1, 2 * hd_b)
        are, aim, bbre, bbim = _s5_prep(s5_lambda_re[l], s5_lambda_im[l], s5_log_step[l], s5_b_re[l], s5_b_im[l])
        mats = (
            _block_diag(jnp.swapaxes(bbre, 1, 2)), _block_diag(jnp.swapaxes(bbim, 1, 2)),
            _block_diag(jnp.swapaxes(s5_c_re[l], 1, 2)), _block_diag(jnp.swapaxes(s5_c_im[l], 1, 2)),
            are.reshape(1, n_state), aim.reshape(1, n_state),
        )
        d_vec = s5_d[l].reshape(1, w_c)
        b_glu = s5_b_glu[l].reshape(1, w_c)

        z, logf, c, ct = _proj(xp, g_row, w_main, w_f, b_f, gains, tri, seq_len=seq_len, w_a=w_a, w_b=w_b)
        mf = _fox_prompt(z, c, ct, batch=batch, seq_len=seq_len, n_heads=n_fh, hd=hd_a)
        md = _diff_prompt(z, bias_tiles, lam_vecs, sub_gain, batch=batch, seq_len=seq_len, n_heads=n_dh, hd=hd_b,
                          col0=col_d, lambda_init=lambda_init)
        ms, hre, him = _s5_prompt(z, mats, d_vec, s5_w_glu[l], b_glu, batch=batch, seq_len=seq_len, col0=col_s,
                                  w_c=w_c)
        xp_new = _merge(xp, mf, md, ms, w_out_b)
        st_p.append((
            z[:, w_a:2 * w_a].reshape(batch, seq_len, n_fh, hd_a),
            z[:, 2 * w_a:3 * w_a].reshape(batch, seq_len, n_fh, hd_a),
            logf[:, :n_fh].reshape(batch, seq_len, n_fh),
            z[:, col_d + w_b:col_d + 2 * w_b].reshape(batch, seq_len, n_dh, 2 * hd_b),
            z[:, col_d + 2 * w_b:col_d + 3 * w_b].reshape(batch, seq_len, n_dh, 2 * hd_b),
            hre.reshape(batch, g_c, p_state), him.reshape(batch, g_c, p_state),
        ))

        zs, logf_s, _, _ = _proj(xs, g_row, w_main, w_f, b_f, gains, tri, seq_len=n_seq * n_q, w_a=w_a, w_b=w_b)
        head_rows = lambda col, width: zs[:, col:col + width].reshape(n_seq, n_q * width // LANES, LANES)
        new_rows = ([head_rows(i * w_a, w_a) for i in range(4)]
                    + [head_rows(col_d + i * w_b, w_b) for i in range(4)])
        lfn = jnp.pad(logf_s[:, :n_fh].reshape(n_seq, 1, n_q * n_fh), ((0, 0), (0, 0), (0, LANES - n_q * n_fh)))
        mf_s, md_s = _sample_attn(page_table, new_rows, lfn, sbias, (fk_c, fv_c, dk_c, dv_c, se_c, tot_c),
                                  lam_vecs, sub_gain, layer=l, n_q=n_q, n_fh=n_fh, n_dh=n_dh, hd_b=hd_b,
                                  lambda_init=lambda_init)
        u_s = zs[:, col_s:col_s + w_c].reshape(n_seq, n_q * w_c)
        g_s = zs[:, col_s + w_c:col_s + 2 * w_c].reshape(n_seq, n_q * w_c)
        ms_s, hre_s, him_s = _s5_sample(u_s, g_s, state_s5_re[l].reshape(n_seq, n_state),
                                        state_s5_im[l].reshape(n_seq, n_state), mats, d_vec, s5_w_glu[l], b_glu,
                                        n_steps=n_q, w_c=w_c)
        xs_new = _merge(xs, mf_s.reshape(n_seq * n_q, w_a), md_s.reshape(n_seq * n_q, w_b),
                        ms_s.reshape(n_seq * n_q, w_c), w_out_b)
        st_s.append((
            zs[:, w_a:2 * w_a].reshape(n_seq, n_q, n_fh, hd_a),
            zs[:, 2 * w_a:3 * w_a].reshape(n_seq, n_q, n_fh, hd_a),
            logf_s[:, :n_fh].reshape(n_seq, n_q, n_fh),
            zs[:, col_d + w_b:col_d + 2 * w_b].reshape(n_seq, n_q, n_dh, 2 * hd_b),
            zs[:, col_d + 2 * w_b:col_d + 3 * w_b].reshape(n_seq, n_q, n_dh, 2 * hd_b),
            hre_s.reshape(n_seq, g_c, p_state), him_s.reshape(n_seq, g_c, p_state),
        ))
        xp, xs = xp_new, xs_new

    outs_p = [jnp.stack([s[i] for s in st_p]) for i in range(7)]
    outs_s = [jnp.stack([s[i] for s in st_s]) for i in range(7)]
    return (xp.reshape(batch, seq_len, d_model), xs.reshape(n_seq, n_q, d_model), *outs_p, *outs_s)
```

```python
import functools
import math

import jax
import jax.numpy as jnp
from jax import lax
from jax.experimental import pallas as pl
from jax.experimental.pallas import tpu as pltpu

F32 = jnp.float32
BF16 = jnp.bfloat16
HIGHEST = lax.Precision.HIGHEST

EPS = 1e-6
NEG = -1e30
LANES = 128
SUBLANES = 8
PAGE = 128
N_BUCKETS = 32
REL_MAX_DIST = 128
LOG2E = math.log2(math.e)
VMEM_LIMIT = 56 * 1024 * 1024

_NT = (((1,), (1,)), ((), ()))


def _params(n_axes, vmem=VMEM_LIMIT):
    return pltpu.CompilerParams(dimension_semantics=("arbitrary",) * n_axes, vmem_limit_bytes=vmem)


def _silu(x):
    return x / (1.0 + jnp.exp(-x))


def _sigmoid(x):
    return 1.0 / (1.0 + jnp.exp(-x))


def _log_sigmoid(x):
    return jnp.minimum(x, 0.0) - jnp.log1p(jnp.exp(-jnp.abs(x)))


def _gelu_tanh(x):
    return 0.5 * x * (1.0 + jnp.tanh(math.sqrt(2.0 / math.pi) * (x + 0.044715 * (x * x * x))))


CUMSUM_CHUNK = 256


def _proj_body(x_ref, g_ref, w_ref, wf_ref, bf_ref, gains_ref, tri_ref,
               z_ref, logf_ref, c_ref, ct_ref, h_scr, carry_scr, *, tm, tn, seq_tiles, seg):
    m = pl.program_id(0)
    n = pl.program_id(1)

    @pl.when(n == 0)
    def _():
        x = x_ref[...]
        ms = jnp.mean(x * x, axis=-1, keepdims=True)
        hb = (x * lax.rsqrt(ms + EPS) * g_ref[...]).astype(BF16)
        h_scr[...] = hb
        logits = jnp.dot(hb, wf_ref[...], preferred_element_type=F32) + bf_ref[...]
        lf = _log_sigmoid(logits)
        logf_ref[...] = lf

        @pl.when(m % seq_tiles == 0)
        def _():
            carry_scr[...] = jnp.zeros_like(carry_scr)

        carry = carry_scr[0:1, :]
        for j in range(tm // CUMSUM_CHUNK):
            blk = lf[j * CUMSUM_CHUNK:(j + 1) * CUMSUM_CHUNK]
            cs = jnp.dot(tri_ref[...], blk, precision=HIGHEST, preferred_element_type=F32) + carry
            c_ref[j * CUMSUM_CHUNK:(j + 1) * CUMSUM_CHUNK, :] = cs * LOG2E
            carry = cs[CUMSUM_CHUNK - 1:CUMSUM_CHUNK, :]
        carry_scr[...] = jnp.broadcast_to(carry, carry_scr.shape)
        ct_ref[...] = c_ref[...].T[0:SUBLANES, :]

    acc = jnp.dot(h_scr[...], w_ref[...], preferred_element_type=F32)
    lane = lax.broadcasted_iota(jnp.int32, (tm, LANES), 1)

    def norm_full(row):
        gain = gains_ref[row:row + 1, :]
        for j in range(tn // LANES):
            y = acc[:, j * LANES:(j + 1) * LANES]
            ms = jnp.mean(y * y, axis=-1, keepdims=True)
            z_ref[:, j * LANES:(j + 1) * LANES] = y * lax.rsqrt(ms + EPS) * gain

    def norm_half(row):
        gain = gains_ref[row:row + 1, :]
        half = LANES // 2
        for j in range(tn // LANES):
            y = acc[:, j * LANES:(j + 1) * LANES]
            y2 = y * y
            lo = jnp.sum(jnp.where(lane < half, y2, 0.0), axis=-1, keepdims=True)
            hi = jnp.sum(jnp.where(lane >= half, y2, 0.0), axis=-1, keepdims=True)
            ms = jnp.where(lane < half, lo, hi) * (1.0 / half)
            z_ref[:, j * LANES:(j + 1) * LANES] = y * lax.rsqrt(ms + EPS) * gain

    fq0, fk0, fv0, dq0, dk0, dv0 = seg

    @pl.when(n < fk0)
    def _():
        norm_full(0)

    @pl.when((n >= fk0) & (n < fv0))
    def _():
        norm_full(1)

    @pl.when(n == dq0)
    def _():
        norm_half(2)

    @pl.when(n == dk0)
    def _():
        norm_half(3)

    @pl.when(((n >= fv0) & (n < dq0)) | (n >= dv0))
    def _():
        z_ref[...] = acc


def _proj(x, g, w_main, w_f, b_f, gains, tri, *, seq_len, w_a, w_b):
    rows, d = x.shape
    n_cols = w_main.shape[1]
    tm = min(1024, rows)
    tn = 512
    assert rows % tm == 0 and n_cols % tn == 0 and seq_len % tm == 0 and tm % CUMSUM_CHUNK == 0
    assert w_a % tn == 0 and w_b == tn
    seg = (0, w_a // tn, 2 * w_a // tn, 4 * w_a // tn, 4 * w_a // tn + 1, 4 * w_a // tn + 2)
    body = functools.partial(_proj_body, tm=tm, tn=tn, seq_tiles=seq_len // tm, seg=seg)
    return pl.pallas_call(
        body,
        grid=(rows // tm, n_cols // tn),
        in_specs=[
            pl.BlockSpec((tm, d), lambda m, n: (m, 0)),
            pl.BlockSpec((1, d), lambda m, n: (0, 0)),
            pl.BlockSpec((d, tn), lambda m, n: (0, n)),
            pl.BlockSpec((d, LANES), lambda m, n: (0, 0)),
            pl.BlockSpec((1, LANES), lambda m, n: (0, 0)),
            pl.BlockSpec((4, LANES), lambda m, n: (0, 0)),
            pl.BlockSpec((CUMSUM_CHUNK, CUMSUM_CHUNK), lambda m, n: (0, 0)),
        ],
        out_specs=[
            pl.BlockSpec((tm, tn), lambda m, n: (m, n)),
            pl.BlockSpec((tm, LANES), lambda m, n: (m, 0)),
            pl.BlockSpec((tm, LANES), lambda m, n: (m, 0)),
            pl.BlockSpec((SUBLANES, tm), lambda m, n: (0, m)),
        ],
        out_shape=[
            jax.ShapeDtypeStruct((rows, n_cols), F32),
            jax.ShapeDtypeStruct((rows, LANES), F32),
            jax.ShapeDtypeStruct((rows, LANES), F32),
            jax.ShapeDtypeStruct((SUBLANES, rows), F32),
        ],
        scratch_shapes=[pltpu.VMEM((tm, d), BF16), pltpu.VMEM((SUBLANES, LANES), F32)],
        compiler_params=_params(2),
        name="proj",
    )(x, g, w_main, w_f, b_f, gains, tri)


def _fox_body(q_ref, k_ref, v_ref, c_ref, ct_ref, gate_ref, o_ref, m_scr, l_scr, acc_scr, *, n_heads, hd, tq, tk):
    qi = pl.program_id(1)
    ki = pl.program_id(2)
    nk = pl.num_programs(2)

    @pl.when(ki == 0)
    def _():
        m_scr[...] = jnp.full_like(m_scr, NEG)
        l_scr[...] = jnp.zeros_like(l_scr)
        acc_scr[...] = jnp.zeros_like(acc_scr)

    def step(diagonal):
        if diagonal:
            visible = (lax.broadcasted_iota(jnp.int32, (tq, tk), 1) <= lax.broadcasted_iota(jnp.int32, (tq, tk), 0))
        def scores(h):
            sl = slice(h * hd, (h + 1) * hd)
            q = q_ref[:, sl].astype(BF16)
            k = k_ref[:, sl].astype(BF16)
            s = lax.dot_general(q, k, _NT, preferred_element_type=F32) - ct_ref[h:h + 1, :]
            return jnp.where(visible, s, NEG) if diagonal else s

        s_next = scores(0)
        for h in range(n_heads):
            sl = slice(h * hd, (h + 1) * hd)
            s = s_next
            if h + 1 < n_heads:
                s_next = scores(h + 1)
            cq = c_ref[:, h:h + 1]
            m_prev = m_scr[h]
            m_new = jnp.maximum(m_prev, jnp.max(s, axis=-1, keepdims=True) + cq)
            alpha = jnp.exp2(m_prev - m_new)
            p = jnp.exp2(s + (cq - m_new[:, 0:1]))
            l_scr[h] = alpha * l_scr[h] + jnp.sum(p, axis=-1, keepdims=True)
            pv = jnp.dot(p.astype(BF16), v_ref[:, sl].astype(BF16), preferred_element_type=F32)
            acc_scr[:, sl] = alpha * acc_scr[:, sl] + pv
            m_scr[h] = m_new

    @pl.when(ki < qi)
    def _():
        step(False)

    @pl.when(ki == qi)
    def _():
        step(True)

    @pl.when(ki == nk - 1)
    def _():
        for h in range(n_heads):
            sl = slice(h * hd, (h + 1) * hd)
            o = acc_scr[:, sl] / l_scr[h]
            o_ref[:, sl] = (o * _silu(gate_ref[:, sl])).astype(BF16)


def _fox_prompt(z, c, ct, *, batch, seq_len, n_heads, hd):
    w = n_heads * hd
    tq = tk = min(512, seq_len)
    nq = seq_len // tq
    body = functools.partial(_fox_body, n_heads=n_heads, hd=hd, tq=tq, tk=tk)
    kv_row = lambda b, qi, ki: b * nq + jnp.minimum(ki, qi)
    return pl.pallas_call(
        body,
        grid=(batch, nq, nq),
        in_specs=[
            pl.BlockSpec((tq, w), lambda b, qi, ki: (b * nq + qi, 0)),
            pl.BlockSpec((tk, w), lambda b, qi, ki: (kv_row(b, qi, ki), 1)),
            pl.BlockSpec((tk, w), lambda b, qi, ki: (kv_row(b, qi, ki), 2)),
            pl.BlockSpec((tq, LANES), lambda b, qi, ki: (b * nq + qi, 0)),
            pl.BlockSpec((SUBLANES, tk), lambda b, qi, ki: (0, kv_row(b, qi, ki))),
            pl.BlockSpec((tq, w), lambda b, qi, ki: (b * nq + qi, 3)),
        ],
        out_specs=pl.BlockSpec((tq, w), lambda b, qi, ki: (b * nq + qi, 0)),
        out_shape=jax.ShapeDtypeStruct((batch * seq_len, w), BF16),
        scratch_shapes=[
            pltpu.VMEM((n_heads, tq, LANES), F32),
            pltpu.VMEM((n_heads, tq, LANES), F32),
            pltpu.VMEM((tq, w), F32),
        ],
        compiler_params=_params(3),
        name="fox_prompt",
    )(z, z, z, c, ct, z)


def _rel_bucket(d):
    max_exact = N_BUCKETS // 2
    df = jnp.maximum(d, 1).astype(F32)
    large = max_exact + (jnp.log(df / max_exact) / math.log(REL_MAX_DIST / max_exact)
                         * (N_BUCKETS - max_exact)).astype(jnp.int32)
    large = jnp.minimum(large, N_BUCKETS - 1)
    return jnp.where(d < max_exact, d, large)


def _bias_lookup(rel_ref, d, h):
    bucket = _rel_bucket(jnp.maximum(d, 0))
    val = jnp.full(d.shape, rel_ref[0, h], F32)
    for b in range(1, N_BUCKETS):
        val = jnp.where(bucket == b, rel_ref[b, h], val)
    return jnp.where(d >= 0, val * LOG2E, NEG)


def _bias_tiles_body(rel_ref, o_ref, *, tq, tk):
    h = pl.program_id(0)
    kind = pl.program_id(1)
    row = lax.broadcasted_iota(jnp.int32, (tq, tk), 0)
    col = lax.broadcasted_iota(jnp.int32, (tq, tk), 1)
    o_ref[...] = _bias_lookup(rel_ref, kind * tq + row - col, h)


def _bias_tiles(rel_bias, *, n_heads, tq, tk):
    return pl.pallas_call(
        functools.partial(_bias_tiles_body, tq=tq, tk=tk),
        grid=(n_heads, 3),
        in_specs=[pl.BlockSpec(memory_space=pltpu.SMEM)],
        out_specs=pl.BlockSpec((None, None, tq, tk), lambda h, kind: (h, kind, 0, 0)),
        out_shape=jax.ShapeDtypeStruct((n_heads, 3, tq, tk), F32),
        compiler_params=_params(2),
        name="bias_tiles",
    )(rel_bias)


def _sample_bias_body(rel_ref, o_ref, *, n_q, n_heads):
    kind = pl.program_id(0)
    rows = 2 * n_q * n_heads
    width = PAGE * n_heads
    r = lax.broadcasted_iota(jnp.int32, (rows, width), 0)
    lane = lax.broadcasted_iota(jnp.int32, (rows, width), 1)
    i = (r % (n_q * n_heads)) // n_heads
    pos = lane // n_heads
    d_last = PAGE + i - pos
    d_new = jnp.where(pos < n_q, i - pos, -1)
    d_far = jnp.full((rows, width), 2 * PAGE, jnp.int32)
    d = jnp.where(kind == 0, d_last, jnp.where(kind == 1, d_new, d_far))
    out = jnp.full((rows, width), NEG, F32)
    for h in range(n_heads):
        own = (lane % n_heads == h) & (r % n_heads == h)
        out = jnp.where(own, _bias_lookup(rel_ref, d, h), out)
    o_ref[...] = out


def _sample_bias(rel_bias, *, n_heads, n_q):
    rows = 2 * n_q * n_heads
    width = PAGE * n_heads
    return pl.pallas_call(
        functools.partial(_sample_bias_body, n_q=n_q, n_heads=n_heads),
        grid=(3,),
        in_specs=[pl.BlockSpec(memory_space=pltpu.SMEM)],
        out_specs=pl.BlockSpec((None, rows, width), lambda kind: (kind, 0, 0)),
        out_shape=jax.ShapeDtypeStruct((3, rows, width), F32),
        compiler_params=_params(1),
        name="sample_bias",
    )(rel_bias)


def _diff_lambda(lam_ref, lambda_init):
    lv = lam_ref[...]
    e1 = jnp.exp(jnp.sum(lv[0:1, :] * lv[1:2, :], axis=-1, keepdims=True))
    e2 = jnp.exp(jnp.sum(lv[2:3, :] * lv[3:4, :], axis=-1, keepdims=True))
    return e1 - e2 + lambda_init


def _diff_body(q_ref, k_ref, v_ref, bias_ref, gate_ref, lam_ref, sub_ref, o_ref,
               m_scr, l_scr, acc_scr, *, n_heads, hd, tq, tk, lambda_init):
    qi = pl.program_id(1)
    ki = pl.program_id(2)
    nk = pl.num_programs(2)
    hw = 2 * hd

    @pl.when(ki == 0)
    def _():
        m_scr[...] = jnp.full_like(m_scr, NEG)
        l_scr[...] = jnp.zeros_like(l_scr)
        acc_scr[...] = jnp.zeros_like(acc_scr)

    @pl.when(ki <= qi)
    def _():
        lane = lax.broadcasted_iota(jnp.int32, (tq, hw), 1)

        for h in range(n_heads):
            sl = slice(h * hw, (h + 1) * hw)
            q = q_ref[:, sl]
            k = k_ref[:, sl].astype(BF16)
            v = v_ref[:, sl].astype(BF16)
            bias = bias_ref[h]
            for mp in range(2):
                idx = 2 * h + mp
                qm = jnp.where((lane < hd) if mp == 0 else (lane >= hd), q, 0.0).astype(BF16)
                s = lax.dot_general(qm, k, _NT, preferred_element_type=F32) + bias
                m_prev = m_scr[idx]
                m_new = jnp.maximum(m_prev, jnp.max(s, axis=-1, keepdims=True))
                alpha = jnp.exp2(m_prev - m_new)
                p = jnp.exp2(s - m_new[:, 0:1])
                l_scr[idx] = alpha * l_scr[idx] + jnp.sum(p, axis=-1, keepdims=True)
                pv = jnp.dot(p.astype(BF16), v, preferred_element_type=F32)
                acc_scr[idx] = alpha * acc_scr[idx] + pv
                m_scr[idx] = m_new

    @pl.when(ki == nk - 1)
    def _():
        lam = _diff_lambda(lam_ref, lambda_init)
        for h in range(n_heads):
            sl = slice(h * hw, (h + 1) * hw)
            o = acc_scr[2 * h] / l_scr[2 * h] - lam * (acc_scr[2 * h + 1] / l_scr[2 * h + 1])
            ms = jnp.mean(o * o, axis=-1, keepdims=True)
            o = o * lax.rsqrt(ms + EPS) * sub_ref[...] * (1.0 - lambda_init)
            o_ref[:, sl] = (o * _silu(gate_ref[:, sl])).astype(BF16)


def _diff_prompt(z, bias_tiles, lam_vecs, sub_gain, *, batch, seq_len, n_heads, hd, col0, lambda_init):
    hw = 2 * hd
    w = n_heads * hw
    tq = tk = min(512, seq_len)
    nq = seq_len // tq
    cb = col0 // w
    body = functools.partial(_diff_body, n_heads=n_heads, hd=hd, tq=tq, tk=tk, lambda_init=lambda_init)
    kv_row = lambda b, qi, ki: b * nq + jnp.minimum(ki, qi)
    kind = lambda b, qi, ki: jnp.clip(qi - jnp.minimum(ki, qi), 0, 2)
    return pl.pallas_call(
        body,
        grid=(batch, nq, nq),
        in_specs=[
            pl.BlockSpec((tq, w), lambda b, qi, ki: (b * nq + qi, cb)),
            pl.BlockSpec((tk, w), lambda b, qi, ki: (kv_row(b, qi, ki), cb + 1)),
            pl.BlockSpec((tk, w), lambda b, qi, ki: (kv_row(b, qi, ki), cb + 2)),
            pl.BlockSpec((n_heads, None, tq, tk), lambda b, qi, ki: (0, kind(b, qi, ki), 0, 0)),
            pl.BlockSpec((tq, w), lambda b, qi, ki: (b * nq + qi, cb + 3)),
            pl.BlockSpec((4, hd), lambda b, qi, ki: (0, 0)),
            pl.BlockSpec((1, hw), lambda b, qi, ki: (0, 0)),
        ],
        out_specs=pl.BlockSpec((tq, w), lambda b, qi, ki: (b * nq + qi, 0)),
        out_shape=jax.ShapeDtypeStruct((batch * seq_len, w), BF16),
        scratch_shapes=[
            pltpu.VMEM((2 * n_heads, tq, LANES), F32),
            pltpu.VMEM((2 * n_heads, tq, LANES), F32),
            pltpu.VMEM((2 * n_heads, tq, hw), F32),
        ],
        compiler_params=_params(3),
        name="diff_prompt",
    )(z, z, z, bias_tiles, z, lam_vecs, sub_gain)


def _s5_prep_body(lre_ref, lim_ref, ls_ref, bre_ref, bim_ref, are_ref, aim_ref, bbre_ref, bbim_ref):
    lr = lre_ref[...]
    li = lim_ref[...]
    step = jnp.exp(ls_ref[...])
    mag = jnp.exp(lr * step)
    ar = mag * jnp.cos(li * step)
    ai = mag * jnp.sin(li * step)
    den = lr * lr + li * li
    nr = ar - 1.0
    cr = (nr * lr + ai * li) / den
    ci = (ai * lr - nr * li) / den
    br = bre_ref[...]
    bi = bim_ref[...]
    are_ref[...] = ar
    aim_ref[...] = ai
    bbre_ref[...] = cr * br - ci * bi
    bbim_ref[...] = cr * bi + ci * br


def _s5_prep(lam_re, lam_im, log_step, b_re, b_im):
    g, p, c = b_re.shape
    col = lambda a: a.reshape(g, p, 1)
    ls = jnp.broadcast_to(log_step.reshape(g, 1, 1), (g, p, 1))
    full3 = lambda shp: pl.BlockSpec(shp, lambda: (0, 0, 0))
    return pl.pallas_call(
        _s5_prep_body,
        in_specs=[full3((g, p, 1))] * 3 + [full3((g, p, c))] * 2,
        out_specs=[full3((g, p, 1))] * 2 + [full3((g, p, c))] * 2,
        out_shape=[jax.ShapeDtypeStruct((g, p, 1), F32)] * 2 + [jax.ShapeDtypeStruct((g, p, c), F32)] * 2,
        name="s5_prep",
    )(col(lam_re), col(lam_im), ls, b_re, b_im)


def _block_diag(blocks):
    g, r, c = blocks.shape
    eye = jnp.eye(g, dtype=blocks.dtype)
    return (blocks[:, :, None, :] * eye[:, None, :, None]).reshape(g * r, g * c)


S5_LANE_CHUNK = 256
S5_ROW_BLOCK = 16


def _s5_scan_body(u_ref, gate_ref, bre_ref, bim_ref, cre_ref, cim_ref, are_ref, aim_ref, d_ref, wg_ref, bg_ref,
                  o_ref, hre_ref, him_ref, pwr_scr, pwi_scr, hr_scr, hi_scr, cr_scr, ci_scr, *, tm, n_state):
    b = pl.program_id(0)
    i = pl.program_id(1)
    nt = pl.num_programs(1)
    blk = S5_ROW_BLOCK
    n_levels = blk.bit_length() - 1

    @pl.when((b == 0) & (i == 0))
    def _():
        pwr_scr[0:1, :] = are_ref[...]
        pwi_scr[0:1, :] = aim_ref[...]
        for lv in range(n_levels):
            s = 1 << lv
            tr = pwr_scr[s - 1:s, :]
            ti = pwi_scr[s - 1:s, :]
            xr = pwr_scr[0:s, :]
            xi = pwi_scr[0:s, :]
            pwr_scr[s:2 * s, :] = xr * tr - xi * ti
            pwi_scr[s:2 * s, :] = xr * ti + xi * tr

    @pl.when(i == 0)
    def _():
        cr_scr[...] = jnp.zeros_like(cr_scr)
        ci_scr[...] = jnp.zeros_like(ci_scr)

    u = u_ref[...]
    ub = u.astype(BF16)
    hr_scr[...] = jnp.dot(ub, bre_ref[...], preferred_element_type=F32)
    hi_scr[...] = jnp.dot(ub, bim_ref[...], preferred_element_type=F32)

    row = lax.broadcasted_iota(jnp.int32, (blk, S5_LANE_CHUNK), 0)

    def chunk(j, carry):
        ln = pl.ds(pl.multiple_of(j * S5_LANE_CHUNK, S5_LANE_CHUNK), S5_LANE_CHUNK)
        pr = cr_scr[0:1, ln]
        pi = ci_scr[0:1, ln]
        wr = pwr_scr[:, ln]
        wi = pwi_scr[:, ln]
        for rb in range(tm // blk):
            rows = slice(rb * blk, (rb + 1) * blk)
            hr = hr_scr[rows, ln]
            hi = hi_scr[rows, ln]
            for lv in range(n_levels):
                s = 1 << lv
                mr = wr[s - 1:s, :]
                mi = wi[s - 1:s, :]
                sr = jnp.where(row >= s, pltpu.roll(hr, s, 0), 0.0)
                si = jnp.where(row >= s, pltpu.roll(hi, s, 0), 0.0)
                hr, hi = hr + (mr * sr - mi * si), hi + (mr * si + mi * sr)
            hr, hi = hr + (wr * pr - wi * pi), hi + (wr * pi + wi * pr)
            hr_scr[rows, ln] = hr
            hi_scr[rows, ln] = hi
            pr = hr[blk - 1:blk, :]
            pi = hi[blk - 1:blk, :]
        cr_scr[:, ln] = jnp.broadcast_to(pr, (SUBLANES, S5_LANE_CHUNK))
        ci_scr[:, ln] = jnp.broadcast_to(pi, (SUBLANES, S5_LANE_CHUNK))
        return carry

    lax.fori_loop(0, n_state // S5_LANE_CHUNK, chunk, 0)

    y = (jnp.dot(hr_scr[...].astype(BF16), cre_ref[...], preferred_element_type=F32)
         - jnp.dot(hi_scr[...].astype(BF16), cim_ref[...], preferred_element_type=F32)
         + d_ref[...] * u)
    zz = _gelu_tanh(y)
    glu = zz * _sigmoid(jnp.dot(zz.astype(BF16), wg_ref[...], preferred_element_type=F32) + bg_ref[...])
    o_ref[...] = (glu * _silu(gate_ref[...])).astype(BF16)

    @pl.when(i == nt - 1)
    def _():
        hre_ref[...] = cr_scr[0:1, :]
        him_ref[...] = ci_scr[0:1, :]


def _s5_prompt(z, mats, d_vec, w_glu, b_glu, *, batch, seq_len, col0, w_c):
    bre, bim, cre, cim, are, aim = mats
    n_state = bre.shape[1]
    tm = min(512, seq_len)
    nt = seq_len // tm
    cb = col0 // w_c
    body = functools.partial(_s5_scan_body, tm=tm, n_state=n_state)
    const = lambda shp: pl.BlockSpec(shp, lambda b, i: (0,) * len(shp))
    return pl.pallas_call(
        body,
        grid=(batch, nt),
        in_specs=[
            pl.BlockSpec((tm, w_c), lambda b, i: (b * nt + i, cb)),
            pl.BlockSpec((tm, w_c), lambda b, i: (b * nt + i, cb + 1)),
            const((w_c, n_state)), const((w_c, n_state)),
            const((n_state, w_c)), const((n_state, w_c)),
            const((1, n_state)), const((1, n_state)),
            const((1, w_c)), const((w_c, w_c)), const((1, w_c)),
        ],
        out_specs=[
            pl.BlockSpec((tm, w_c), lambda b, i: (b * nt + i, 0)),
            pl.BlockSpec((None, 1, n_state), lambda b, i: (b, 0, 0)),
            pl.BlockSpec((None, 1, n_state), lambda b, i: (b, 0, 0)),
        ],
        out_shape=[
            jax.ShapeDtypeStruct((batch * seq_len, w_c), BF16),
            jax.ShapeDtypeStruct((batch, 1, n_state), F32),
            jax.ShapeDtypeStruct((batch, 1, n_state), F32),
        ],
        scratch_shapes=[
            pltpu.VMEM((S5_ROW_BLOCK, n_state), F32), pltpu.VMEM((S5_ROW_BLOCK, n_state), F32),
            pltpu.VMEM((tm, n_state), F32), pltpu.VMEM((tm, n_state), F32),
            pltpu.VMEM((SUBLANES, n_state), F32), pltpu.VMEM((SUBLANES, n_state), F32),
        ],
        compiler_params=_params(2),
        name="s5_prompt",
    )(z, z, bre.astype(BF16), bim.astype(BF16), cre.astype(BF16), cim.astype(BF16), are, aim,
      d_vec, w_glu.astype(BF16), b_glu)


def _s5_sample_body(u_ref, gate_ref, h0r_ref, h0i_ref, bre_ref, bim_ref, cre_ref, cim_ref, are_ref, aim_ref,
                    d_ref, wg_ref, bg_ref, o_ref, hre_ref, him_ref, *, n_steps, w_c):
    hr = h0r_ref[...]
    hi = h0i_ref[...]
    ar = are_ref[...]
    ai = aim_ref[...]
    for t in range(n_steps):
        sl = slice(t * w_c, (t + 1) * w_c)
        u = u_ref[:, sl]
        bur = jnp.dot(u, bre_ref[...], precision=HIGHEST, preferred_element_type=F32)
        bui = jnp.dot(u, bim_ref[...], precision=HIGHEST, preferred_element_type=F32)
        hr, hi = ar * hr - ai * hi + bur, ar * hi + ai * hr + bui
        y = (jnp.dot(hr.astype(BF16), cre_ref[...], preferred_element_type=F32)
             - jnp.dot(hi.astype(BF16), cim_ref[...], preferred_element_type=F32)
             + d_ref[...] * u)
        zz = _gelu_tanh(y)
        glu = zz * _sigmoid(jnp.dot(zz.astype(BF16), wg_ref[...], preferred_element_type=F32) + bg_ref[...])
        o_ref[:, sl] = (glu * _silu(gate_ref[:, sl])).astype(BF16)
    hre_ref[...] = hr
    him_ref[...] = hi


def _s5_sample(u, gate, h0r, h0i, mats, d_vec, w_glu, b_glu, *, n_steps, w_c):
    bre, bim, cre, cim, are, aim = mats
    n_seq = u.shape[0]
    n_state = bre.shape[1]
    body = functools.partial(_s5_sample_body, n_steps=n_steps, w_c=w_c)
    return pl.pallas_call(
        body,
        out_shape=[
            jax.ShapeDtypeStruct((n_seq, n_steps * w_c), BF16),
            jax.ShapeDtypeStruct((n_seq, n_state), F32),
            jax.ShapeDtypeStruct((n_seq, n_state), F32),
        ],
        compiler_params=pltpu.CompilerParams(vmem_limit_bytes=VMEM_LIMIT),
        name="s5_sample",
    )(u, gate, h0r, h0i, bre, bim, cre.astype(BF16), cim.astype(BF16), are, aim, d_vec, w_glu.astype(BF16), b_glu)


def _merge_body(x_ref, mf_ref, md_ref, ms_ref, w_ref, o_ref, *, w_a, w_b):
    acc = jnp.dot(mf_ref[...].astype(BF16), w_ref[0:w_a, :], preferred_element_type=F32)
    acc += jnp.dot(md_ref[...].astype(BF16), w_ref[w_a:w_a + w_b, :], preferred_element_type=F32)
    acc += jnp.dot(ms_ref[...].astype(BF16), w_ref[w_a + w_b:, :], preferred_element_type=F32)
    o_ref[...] = x_ref[...] + acc


def _merge(x, mf, md, ms, w_out_bf16):
    rows, d = x.shape
    w_a, w_b, w_c = mf.shape[1], md.shape[1], ms.shape[1]
    tm = min(512, rows)
    body = functools.partial(_merge_body, w_a=w_a, w_b=w_b)
    return pl.pallas_call(
        body,
        grid=(rows // tm,),
        in_specs=[
            pl.BlockSpec((tm, d), lambda m: (m, 0)),
            pl.BlockSpec((tm, w_a), lambda m: (m, 0)),
            pl.BlockSpec((tm, w_b), lambda m: (m, 0)),
            pl.BlockSpec((tm, w_c), lambda m: (m, 0)),
            pl.BlockSpec((w_a + w_b + w_c, d), lambda m: (0, 0)),
        ],
        out_specs=pl.BlockSpec((tm, d), lambda m: (m, 0)),
        out_shape=jax.ShapeDtypeStruct((rows, d), F32),
        compiler_params=_params(1),
        name="merge",
    )(x, mf, md, ms, w_out_bf16)


def _logf_suffix_body(x_ref, se_ref, tot_ref, *, n_heads):
    x = x_ref[...]
    width = x.shape[1]
    lane = lax.broadcasted_iota(jnp.int32, x.shape, 1)
    si = x
    tot = x
    s = n_heads
    while s < width:
        si = si + jnp.where(lane < width - s, pltpu.roll(si, width - s, 1), 0.0)
        tot = tot + pltpu.roll(tot, s, 1)
        s *= 2
    se_ref[...] = (si - x) * LOG2E
    tot_ref[...] = tot * LOG2E


def _logf_suffix(logf_pages, *, n_heads):
    rows, width = logf_pages.shape
    tr = max(t for t in range(SUBLANES, 256 + 1, SUBLANES) if rows % t == 0)
    return pl.pallas_call(
        functools.partial(_logf_suffix_body, n_heads=n_heads),
        grid=(rows // tr,),
        in_specs=[pl.BlockSpec((tr, width), lambda r: (r, 0))],
        out_specs=[pl.BlockSpec((tr, width), lambda r: (r, 0))] * 2,
        out_shape=[jax.ShapeDtypeStruct((rows, width), F32)] * 2,
        compiler_params=_params(1),
        name="logf_suffix",
    )(logf_pages)


def _softmax_block(scores, values, m_scr, l_scr, acc_scr):
    m_prev = m_scr[...]
    m_cur = functools.reduce(jnp.maximum, [jnp.max(s, axis=-1, keepdims=True) for s in scores])
    m_new = jnp.maximum(m_prev, m_cur)
    alpha = jnp.exp2(m_prev - m_new)
    l_new = alpha * l_scr[...]
    acc = alpha * acc_scr[...]
    for s, v in zip(scores, values):
        p = jnp.exp2(s - m_new[:, 0:1])
        l_new = l_new + jnp.sum(p, axis=-1, keepdims=True)
        acc = acc + jnp.dot(p, v, preferred_element_type=F32)
    m_scr[...] = m_new
    l_scr[...] = l_new
    acc_scr[...] = acc


def _sample_attn_body(pt_ref, *refs, n_q, pps, n_fh, n_dh, hd_b, lambda_init):
    del pt_ref
    n_in = 10 + 6 * pps
    (fq_ref, fkn_ref, fvn_ref, fg_ref, dq_ref, dkn_ref, dvn_ref, dg_ref, lfn_ref, sb_ref) = refs[:10]
    page_refs = refs[10:n_in]
    lam_ref, sub_ref, of_ref, od_ref = refs[n_in:n_in + 4]
    (qd_scr, cq_scr, r_scr, fkn_scr, fvn_scr, dkn_scr, dvn_scr,
     fm_scr, fl_scr, facc_scr, dm_scr, dl_scr, dacc_scr) = refs[n_in + 4:]
    j = pl.program_id(1)
    nj = pl.num_programs(1)
    rf = n_q * n_fh
    rd = n_q * n_dh
    wf = PAGE * n_fh

    @pl.when(j == 0)
    def _():
        fm_scr[...] = jnp.full_like(fm_scr, NEG)
        fl_scr[...] = jnp.zeros_like(fl_scr)
        facc_scr[...] = jnp.zeros_like(facc_scr)
        dm_scr[...] = jnp.full_like(dm_scr, NEG)
        dl_scr[...] = jnp.zeros_like(dl_scr)
        dacc_scr[...] = jnp.zeros_like(dacc_scr)
        r_scr[...] = jnp.zeros_like(r_scr)
        dq = dq_ref[...]
        dlane = lax.broadcasted_iota(jnp.int32, (rd, LANES), 1)
        qd_scr[0:rd, :] = jnp.where(dlane < hd_b, dq, 0.0)
        qd_scr[rd:2 * rd, :] = jnp.where(dlane >= hd_b, dq, 0.0)
        fkn_scr[...] = jnp.zeros_like(fkn_scr)
        fvn_scr[...] = jnp.zeros_like(fvn_scr)
        dkn_scr[...] = jnp.zeros_like(dkn_scr)
        dvn_scr[...] = jnp.zeros_like(dvn_scr)
        fkn_scr[0:rf, :] = fkn_ref[...]
        fvn_scr[0:rf, :] = fvn_ref[...]
        dkn_scr[0:rd, :] = dkn_ref[...]
        dvn_scr[0:rd, :] = dvn_ref[...]
        ln = lax.broadcasted_iota(jnp.int32, (1, LANES), 1)
        cn = lfn_ref[...]
        s = n_fh
        while s < rf:
            cn = cn + jnp.where(ln >= s, pltpu.roll(cn, s, 1), 0.0)
            s *= 2
        cn = cn * LOG2E
        rr = lax.broadcasted_iota(jnp.int32, (rf, LANES), 0)
        ll = lax.broadcasted_iota(jnp.int32, (rf, LANES), 1)
        cq = jnp.sum(jnp.where(ll == rr, cn, 0.0), axis=-1, keepdims=True)
        cq_scr[...] = jnp.broadcast_to(cq, (rf, LANES))
        ok = (ll < rf) & (ll % n_fh == rr % n_fh) & (ll // n_fh <= rr // n_fh)
        sf = lax.dot_general(fq_ref[...], fkn_scr[...], _NT, preferred_element_type=F32)
        sf = jnp.where(ok, sf + cq - cn, NEG)
        _softmax_block([sf], [fvn_scr[...]], fm_scr, fl_scr, facc_scr)
        sd = lax.dot_general(qd_scr[...], dkn_scr[...], _NT, preferred_element_type=F32)
        sd = sd + sb_ref[1][:, 0:LANES]
        _softmax_block([sd], [dvn_scr[...]], dm_scr, dl_scr, dacc_scr)

    frow = lax.broadcasted_iota(jnp.int32, (rf, wf), 0)
    flane = lax.broadcasted_iota(jnp.int32, (rf, wf), 1)
    own = (flane % n_fh) == (frow % n_fh)
    cq = cq_scr[:, 0:1]
    fq = fq_ref[...]
    qd = qd_scr[...]
    rsum = r_scr[...]
    first_kind = jnp.where(j == 0, 0, 2)
    f_scores, f_values, d_scores, d_values = [], [], [], []
    for g in range(pps):
        fk_ref, fv_ref, dk_ref, dv_ref, se_ref, tot_ref = page_refs[6 * g:6 * g + 6]
        sf = lax.dot_general(fq, fk_ref[...], _NT, preferred_element_type=F32)
        f_scores.append(jnp.where(own, sf + cq + (se_ref[...] + rsum), NEG))
        f_values.append(fv_ref[...])
        rsum = rsum + tot_ref[...]
        sbias = sb_ref[first_kind] if g == 0 else sb_ref[2]
        sd = lax.dot_general(qd, dk_ref[...], _NT, preferred_element_type=F32)
        d_scores.append(sd + sbias)
        d_values.append(dv_ref[...])
    r_scr[...] = rsum
    _softmax_block(f_scores, f_values, fm_scr, fl_scr, facc_scr)
    _softmax_block(d_scores, d_values, dm_scr, dl_scr, dacc_scr)

    @pl.when(j == nj - 1)
    def _():
        of_ref[...] = (facc_scr[...] / fl_scr[...]) * _silu(fg_ref[...])
        lam = _diff_lambda(lam_ref, lambda_init)
        o = dacc_scr[...] / dl_scr[...]
        od = o[0:rd, :] - lam * o[rd:2 * rd, :]
        ms = jnp.mean(od * od, axis=-1, keepdims=True)
        od = od * lax.rsqrt(ms + EPS) * sub_ref[...] * (1.0 - lambda_init)
        od_ref[...] = od * _silu(dg_ref[...])


def _sample_attn(page_table, new_rows, lfn, sbias, caches, lam_vecs, sub_gain, *, layer, n_q, n_fh, n_dh, hd_b,
                 lambda_init):
    fk_c, fv_c, dk_c, dv_c, se_c, tot_c = caches
    fq, fkn, fvn, fg, dq, dkn, dvn, dg = new_rows
    n_seq = fq.shape[0]
    n_pages = page_table.shape[1]
    pps = max(p for p in (8, 4, 2, 1) if n_pages % p == 0)
    nj = n_pages // pps
    rf = n_q * n_fh
    rd = n_q * n_dh
    wf = PAGE * n_fh
    wd = PAGE * n_dh
    body = functools.partial(_sample_attn_body, n_q=n_q, pps=pps, n_fh=n_fh, n_dh=n_dh, hd_b=hd_b,
                             lambda_init=lambda_init)
    seq_spec = lambda r: pl.BlockSpec((None, r, LANES), lambda s, j, pt: (s, 0, 0))

    def page_specs(g):
        idx = lambda s, j, pt: (layer, pt[s, n_pages - 1 - (j * pps + g)], 0, 0)
        return [
            pl.BlockSpec((None, None, wf, LANES), idx), pl.BlockSpec((None, None, wf, LANES), idx),
            pl.BlockSpec((None, None, wd, LANES), idx), pl.BlockSpec((None, None, wd, LANES), idx),
            pl.BlockSpec((None, None, 1, wf), idx), pl.BlockSpec((None, None, 1, wf), idx),
        ]

    in_specs = [seq_spec(rf)] * 4 + [seq_spec(rd)] * 4 + [
        seq_spec(1),
        pl.BlockSpec(sbias.shape, lambda s, j, pt: (0, 0, 0)),
    ]
    args = [fq, fkn, fvn, fg, dq, dkn, dvn, dg, lfn, sbias]
    for g in range(pps):
        in_specs += page_specs(g)
        args += [fk_c, fv_c, dk_c, dv_c, se_c, tot_c]
    in_specs += [pl.BlockSpec((4, hd_b), lambda s, j, pt: (0, 0)), pl.BlockSpec((1, 2 * hd_b), lambda s, j, pt: (0, 0))]
    args += [lam_vecs, sub_gain]
    grid_spec = pltpu.PrefetchScalarGridSpec(
        num_scalar_prefetch=1,
        grid=(n_seq, nj),
        in_specs=in_specs,
        out_specs=[seq_spec(rf), seq_spec(rd)],
        scratch_shapes=[
            pltpu.VMEM((2 * rd, LANES), F32),
            pltpu.VMEM((rf, LANES), F32),
            pltpu.VMEM((1, wf), F32),
            pltpu.VMEM((PAGE, LANES), F32), pltpu.VMEM((PAGE, LANES), F32),
            pltpu.VMEM((PAGE, LANES), F32), pltpu.VMEM((PAGE, LANES), F32),
            pltpu.VMEM((rf, LANES), F32), pltpu.VMEM((rf, LANES), F32), pltpu.VMEM((rf, LANES), F32),
            pltpu.VMEM((2 * rd, LANES), F32), pltpu.VMEM((2 * rd, LANES), F32), pltpu.VMEM((2 * rd, LANES), F32),
        ],
    )
    return pl.pallas_call(
        body,
        grid_spec=grid_spec,
        out_shape=[
            jax.ShapeDtypeStruct((n_seq, rf, LANES), F32),
            jax.ShapeDtypeStruct((n_seq, rd, LANES), F32),
        ],
        compiler_params=_params(2),
        name="sample_attn",
    )(page_table, *args)


def kernel(x_prompt, x_sample, cache_fox_k, cache_fox_v, cache_fox_logf, cache_diff_k, cache_diff_v, state_s5_re, state_s5_im, page_table, norm_gain, w_in, fox_b_f, fox_q_gain, fox_k_gain, diff_q_gain, diff_k_gain, diff_lambda_q1, diff_lambda_k1, diff_lambda_q2, diff_lambda_k2, diff_subln_gain, rel_bias, s5_lambda_re, s5_lambda_im, s5_b_re, s5_b_im, s5_c_re, s5_c_im, s5_d, s5_log_step, s5_w_glu, s5_b_glu, w_out):
    batch, seq_len, d_model = x_prompt.shape
    n_seq, n_q, _ = x_sample.shape
    depth, n_pool, page, n_fh, hd_a = cache_fox_k.shape
    n_dh = cache_diff_k.shape[3]
    hd_b = cache_diff_k.shape[4] // 2
    g_c, p_state, c_group = s5_b_re.shape[1:]
    assert page == PAGE and n_fh == SUBLANES and hd_a == LANES and 2 * hd_b == LANES
    w_a = n_fh * hd_a
    w_b = n_dh * 2 * hd_b
    w_c = g_c * c_group
    n_state = g_c * p_state
    col_d = 4 * w_a
    col_s = col_d + 4 * w_b

    xp = x_prompt.reshape(batch * seq_len, d_model)
    xs = x_sample.reshape(n_seq * n_q, d_model)

    fk_c = cache_fox_k.reshape(depth, n_pool, PAGE * n_fh, hd_a)
    fv_c = cache_fox_v.reshape(depth, n_pool, PAGE * n_fh, hd_a)
    dk_c = cache_diff_k.reshape(depth, n_pool, PAGE * n_dh, 2 * hd_b)
    dv_c = cache_diff_v.reshape(depth, n_pool, PAGE * n_dh, 2 * hd_b)
    se_c, tot_c = _logf_suffix(cache_fox_logf.reshape(depth * n_pool, PAGE * n_fh), n_heads=n_fh)
    se_c = se_c.reshape(depth, n_pool, 1, PAGE * n_fh)
    tot_c = tot_c.reshape(depth, n_pool, 1, PAGE * n_fh)

    tile = min(512, seq_len)
    bias_tiles = _bias_tiles(rel_bias, n_heads=n_dh, tq=tile, tk=tile)
    sbias = _sample_bias(rel_bias, n_heads=n_dh, n_q=n_q)
    tri = jnp.tril(jnp.ones((CUMSUM_CHUNK, CUMSUM_CHUNK), F32))

    st_p, st_s = [], []
    for l in range(depth):
        lambda_init = 0.8 - 0.6 * math.exp(-0.3 * l)
        w = w_in[l]
        w_main = jnp.concatenate([w[:, :3 * w_a], w[:, 3 * w_a + n_fh:]], axis=1).astype(BF16)
        w_f = jnp.pad(w[:, 3 * w_a:3 * w_a + n_fh], ((0, 0), (0, LANES - n_fh))).astype(BF16)
        b_f = jnp.pad(fox_b_f[l], (0, LANES - n_fh)).reshape(1, LANES)
        gains = jnp.stack([fox_q_gain[l] * (hd_a ** -0.5 * LOG2E), fox_k_gain[l],
                           jnp.tile(diff_q_gain[l], 2) * (hd_b ** -0.5 * LOG2E), jnp.tile(diff_k_gain[l], 2)])
        g_row = norm_gain[l].reshape(1, d_model)
        w_out_b = w_out[l].astype(BF16)
        lam_vecs = jnp.stack([diff_lambda_q1[l], diff_lambda_k1[l], diff_lambda_q2[l], diff_lambda_k2[l]])
        sub_gain = diff_subln_gain[l].reshape(1, 2 * hd_b)
        are, aim, bbre, bbim = _s5_prep(s5_lambda_re[l], s5_lambda_im[l], s5_log_step[l], s5_b_re[l], s5_b_im[l])
        mats = (
            _block_diag(jnp.swapaxes(bbre, 1, 2)), _block_diag(jnp.swapaxes(bbim, 1, 2)),
            _block_diag(jnp.swapaxes(s5_c_re[l], 1, 2)), _block_diag(jnp.swapaxes(s5_c_im[l], 1, 2)),
            are.reshape(1, n_state), aim.reshape(1, n_state),
        )
        d_vec = s5_d[l].reshape(1, w_c)
        b_glu = s5_b_glu[l].reshape(1, w_c)

        z, logf, c, ct = _proj(xp, g_row, w_main, w_f, b_f, gains, tri, seq_len=seq_len, w_a=w_a, w_b=w_b)
        mf = _fox_prompt(z, c, ct, batch=batch, seq_len=seq_len, n_heads=n_fh, hd=hd_a)
        md = _diff_prompt(z, bias_tiles, lam_vecs, sub_gain, batch=batch, seq_len=seq_len, n_heads=n_dh, hd=hd_b,
                          col0=col_d, lambda_init=lambda_init)
        ms, hre, him = _s5_prompt(z, mats, d_vec, s5_w_glu[l], b_glu, batch=batch, seq_len=seq_len, col0=col_s,
                                  w_c=w_c)
        xp_new = _merge(xp, mf, md, ms, w_out_b)
        st_p.append((
            z[:, w_a:2 * w_a].reshape(batch, seq_len, n_fh, hd_a),
            z[:, 2 * w_a:3 * w_a].reshape(batch, seq_len, n_fh, hd_a),
            logf[:, :n_fh].reshape(batch, seq_len, n_fh),
            z[:, col_d + w_b:col_d + 2 * w_b].reshape(batch, seq_len, n_dh, 2 * hd_b),
            z[:, col_d + 2 * w_b:col_d + 3 * w_b].reshape(batch, seq_len, n_dh, 2 * hd_b),
            hre.reshape(batch, g_c, p_state), him.reshape(batch, g_c, p_state),
        ))

        zs, logf_s, _, _ = _proj(xs, g_row, w_main, w_f, b_f, gains, tri, seq_len=n_seq * n_q, w_a=w_a, w_b=w_b)
        head_rows = lambda col, width: zs[:, col:col + width].reshape(n_seq, n_q * width // LANES, LANES)
        new_rows = ([head_rows(i * w_a, w_a) for i in range(4)]
                    + [head_rows(col_d + i * w_b, w_b) for i in range(4)])
        lfn = jnp.pad(logf_s[:, :n_fh].reshape(n_seq, 1, n_q * n_fh), ((0, 0), (0, 0), (0, LANES - n_q * n_fh)))
        mf_s, md_s = _sample_attn(page_table, new_rows, lfn, sbias, (fk_c, fv_c, dk_c, dv_c, se_c, tot_c),
                                  lam_vecs, sub_gain, layer=l, n_q=n_q, n_fh=n_fh, n_dh=n_dh, hd_b=hd_b,
                                  lambda_init=lambda_init)
        u_s = zs[:, col_s:col_s + w_c].reshape(n_seq, n_q * w_c)
        g_s = zs[:, col_s + w_c:col_s + 2 * w_c].reshape(n_seq, n_q * w_c)
        ms_s, hre_s, him_s = _s5_sample(u_s, g_s, state_s5_re[l].reshape(n_seq, n_state),
                                        state_s5_im[l].reshape(n_seq, n_state), mats, d_vec, s5_w_glu[l], b_glu,
                                        n_steps=n_q, w_c=w_c)
        xs_new = _merge(xs, mf_s.reshape(n_seq * n_q, w_a), md_s.reshape(n_seq * n_q, w_b),
                        ms_s.reshape(n_seq * n_q, w_c), w_out_b)
        st_s.append((
            zs[:, w_a:2 * w_a].reshape(n_seq, n_q, n_fh, hd_a),
            zs[:, 2 * w_a:3 * w_a].reshape(n_seq, n_q, n_fh, hd_a),
            logf_s[:, :n_fh].reshape(n_seq, n_q, n_fh),
            zs[:, col_d + w_b:col_d + 2 * w_b].reshape(n_seq, n_q, n_dh, 2 * hd_b),
            zs[:, col_d + 2 * w_b:col_d + 3 * w_b].reshape(n_seq, n_q, n_dh, 2 * hd_b),
            hre_s.reshape(n_seq, g_c, p_state), him_s.reshape(n_seq, g_c, p_state),
        ))
        xp, xs = xp_new, xs_new

    outs_p = [jnp.stack([s[i] for s in st_p]) for i in range(7)]
    outs_s = [jnp.stack([s[i] for s in st_s]) for i in range(7)]
    return (xp.reshape(batch, seq_len, d_model), xs.reshape(n_seq, n_q, d_model), *outs_p, *outs_s)
```
